```python
import math
import jax, jax.numpy as jnp
from jax import lax
import numpy as np

D_MODEL = 2048
BATCH = 8
SEQ = 2048
DEPTH = 1

D_MIX = D_MODEL
HG_WIDTH = D_MIX // 2
HG_DK = 128
HG_HEADS = HG_WIDTH // HG_DK
HG_DV = HG_WIDTH // HG_HEADS
HG_CHUNK = 64
NSA_WIDTH = D_MIX - HG_WIDTH
NSA_HEAD_DIM = 128
NSA_HEADS = NSA_WIDTH // NSA_HEAD_DIM
NSA_KV_HEADS = 2
NSA_GROUP = NSA_HEADS // NSA_KV_HEADS
KV_WIDTH = NSA_KV_HEADS * NSA_HEAD_DIM
CMP_BLOCK = 32
CMP_STRIDE = 16
SEL_BLOCK = 64
SEL_TOP = 16
WINDOW = 512
Q_BLOCK = 128
SEL_Q_BLOCK = 32
IN_COLS = 4 * HG_WIDTH + NSA_WIDTH + 6 * KV_WIDTH + 3 * NSA_HEADS
N_EXPERTS = 32
TOP_K = 4
D_FF = D_MODEL
SWIGLU_LIMIT = 7.0
SWIGLU_ALPHA = 1.702
MOE_BLOCK = 256
EPS = 1e-6
NEG_INF = -1e30
FORCED_SCORE = 1e4

kernel_name = "hymba_hgrn2_nsa_moe_layer"


def _rmsnorm(x, g):
    xf = x.astype(jnp.float32)
    y = xf * lax.rsqrt(jnp.mean(xf * xf, axis=-1, keepdims=True) + EPS)
    return (y * g.astype(jnp.float32)).astype(x.dtype)


def _alibi_slopes():
    h = np.arange(1, NSA_HEADS + 1, dtype=np.float32)
    slopes = np.power(np.float32(2.0), -8.0 * h / NSA_HEADS).astype(np.float32)
    return jnp.asarray(slopes).reshape(NSA_KV_HEADS, NSA_GROUP)


def _hgrn2(q, f_logit, i, g, lb, norm_g):
    B, T, _ = q.shape
    dt = q.dtype
    f32 = jnp.float32
    f = lb + (1.0 - lb) * jax.nn.sigmoid(f_logit.astype(f32))
    log_f = jnp.log(f)
    k = 1.0 - f
    qf = jax.nn.silu(q.astype(f32))
    v = i.astype(f32)
    nc = T // HG_CHUNK

    def to_chunks(t, d):
        return t.reshape(B, nc, HG_CHUNK, HG_HEADS, d).transpose(1, 0, 3, 2, 4)

    xs = (to_chunks(qf, HG_DK), to_chunks(k, HG_DK), to_chunks(v, HG_DV), to_chunks(log_f, HG_DK))
    causal = jnp.tril(jnp.ones((HG_CHUNK, HG_CHUNK), f32))

    def step(S, c):
        qc, kc, vc, gc = c
        b = jnp.cumsum(gc, axis=-2)
        b_last = b[..., -1:, :]
        q_dec = qc * jnp.exp(b)
        k_dec = kc * jnp.exp(-b)
        a = jnp.einsum('bhtk,bhsk->bhts', q_dec, k_dec) * causal
        o = jnp.einsum('bhts,bhsv->bhtv', a, vc) + jnp.einsum('bhtk,bhkv->bhtv', q_dec, S)
        S = jnp.exp(b_last)[..., 0, :, None] * S + jnp.einsum('bhsk,bhsv->bhkv', kc * jnp.exp(b_last - b), vc)
        return S, o

    S0 = jnp.zeros((B, HG_HEADS, HG_DK, HG_DV), f32)
    _, o = lax.scan(step, S0, xs)
    o = o.transpose(1, 0, 3, 2, 4).reshape(B, T, HG_HEADS, HG_DV)
    o = _rmsnorm(o, norm_g) * jax.nn.silu(g.astype(f32).reshape(B, T, HG_HEADS, HG_DV))
    return o.reshape(B, T, HG_WIDTH).astype(dt)


def _nsa(q, k_cmp, v_cmp, k_sel, v_sel, k_win, v_win, gates, cmp_pos, cmp_w1, cmp_w2, norm_g):
    B, T, _ = q.shape
    dt = q.dtype
    f32 = jnp.float32
    G, R, d = NSA_KV_HEADS, NSA_GROUP, NSA_HEAD_DIM
    scale = d ** -0.5
    slopes = _alibi_slopes()
    qh = q.reshape(B, T, G, R, d).transpose(0, 2, 3, 1, 4)

    def kvh(t):
        return t.reshape(B, T, G, d).transpose(0, 2, 1, 3)

    t_pos = jnp.arange(T, dtype=jnp.int32)

    n_cmp = (T - CMP_BLOCK) // CMP_STRIDE + 1
    blk_idx = np.arange(n_cmp)[:, None] * CMP_STRIDE + np.arange(CMP_BLOCK)[None, :]

    def compress(t, j):
        blocks = kvh(t)[:, :, blk_idx] + cmp_pos[j]
        hdn = jax.nn.gelu(blocks.reshape(B, G, n_cmp, CMP_BLOCK * d) @ cmp_w1[j])
        return hdn @ cmp_w2[j]

    kc = compress(k_cmp, 0)
    vc = compress(v_cmp, 1)
    cmp_end = jnp.asarray(blk_idx[:, -1], jnp.int32)
    dist_c = t_pos[:, None] - cmp_end[None, :]
    valid_c = dist_c >= 0
    s = jnp.einsum('bgrtd,bgnd->bgrtn', qh, kc, preferred_element_type=f32) * scale \
        - slopes[:, :, None, None] * dist_c.astype(f32)
    s = jnp.where(valid_c, s, NEG_INF)
    p_cmp = jnp.where(valid_c, jax.nn.softmax(s, axis=-1), 0.0)
    o_cmp = jnp.einsum('bgrtn,bgnd->bgrtd', p_cmp.astype(dt), vc)

    n_sel = T // SEL_BLOCK
    top_n = min(SEL_TOP, n_sel)
    ratio_s = SEL_BLOCK // CMP_STRIDE
    ratio_c = CMP_BLOCK // CMP_STRIDE
    jj, mm, nn = np.meshgrid(np.arange(n_sel), np.arange(ratio_s), np.arange(ratio_c), indexing='ij')
    ii = ratio_s * jj + mm + nn - (ratio_c - 1)
    ok = (ii >= 0) & (ii < n_cmp)
    overlap = np.zeros((n_cmp, n_sel), np.float32)
    np.add.at(overlap, (ii[ok], jj[ok]), 1.0)
    imp = jnp.einsum('bgrtn,nj->bgtj', p_cmp, jnp.asarray(overlap))
    blk = jnp.arange(n_sel, dtype=jnp.int32)
    cur = t_pos // SEL_BLOCK
    sel_valid = blk[None, :] <= cur[:, None]
    forced = (blk[None, :] == 0) | (blk[None, :] == cur[:, None]) | (blk[None, :] == cur[:, None] - 1)
    score = jnp.where(forced, FORCED_SCORE, jnp.where(sel_valid, imp, -FORCED_SCORE))
    _, sel_idx = lax.top_k(score, top_n)

    k_blocks = kvh(k_sel).reshape(B, G, n_sel, SEL_BLOCK, d)
    v_blocks = kvh(v_sel).reshape(B, G, n_sel, SEL_BLOCK, d)
    gather = jax.vmap(jax.vmap(lambda blocks, ix: blocks[ix]))

    def sel_block(qb):
        s0 = qb * SEL_Q_BLOCK
        qblk = lax.dynamic_slice_in_dim(qh, s0, SEL_Q_BLOCK, axis=3)
        iblk = lax.dynamic_slice_in_dim(sel_idx, s0, SEL_Q_BLOCK, axis=2)
        kb = gather(k_blocks, iblk)
        vb = gather(v_blocks, iblk)
        tq = s0 + jnp.arange(SEL_Q_BLOCK, dtype=jnp.int32)
        kpos = iblk[..., None] * SEL_BLOCK + jnp.arange(SEL_BLOCK, dtype=jnp.int32)
        dist = (tq[:, None, None] - kpos)[:, :, None]
        sc = jnp.einsum('bgrqd,bgqkld->bgrqkl', qblk, kb, preferred_element_type=f32) * scale \
            - slopes[:, :, None, None, None] * dist.astype(f32)
        sc = jnp.where(dist >= 0, sc, NEG_INF)
        p = jax.nn.softmax(sc.reshape(B, G, R, SEL_Q_BLOCK, top_n * SEL_BLOCK), axis=-1).reshape(sc.shape)
        return jnp.einsum('bgrqkl,bgqkld->bgrqd', p.astype(dt), vb)

    o_sel = lax.map(sel_block, jnp.arange(T // SEL_Q_BLOCK, dtype=jnp.int32))
    o_sel = o_sel.transpose(1, 2, 3, 0, 4, 5).reshape(B, G, R, T, d)

    kw = jnp.pad(kvh(k_win), ((0, 0), (0, 0), (WINDOW, 0), (0, 0)))
    vw = jnp.pad(kvh(v_win), ((0, 0), (0, 0), (WINDOW, 0), (0, 0)))
    span = WINDOW + Q_BLOCK

    def win_block(qb):
        s0 = qb * Q_BLOCK
        qblk = lax.dynamic_slice_in_dim(qh, s0, Q_BLOCK, axis=3)
        kblk = lax.dynamic_slice_in_dim(kw, s0, span, axis=2)
        vblk = lax.dynamic_slice_in_dim(vw, s0, span, axis=2)
        tq = s0 + jnp.arange(Q_BLOCK, dtype=jnp.int32)
        kpos = s0 - WINDOW + jnp.arange(span, dtype=jnp.int32)
        dist = tq[:, None] - kpos[None, :]
        okw = (dist >= 0) & (dist < WINDOW) & (kpos[None, :] >= 0)
        sc = jnp.einsum('bgrqd,bgkd->bgrqk', qblk, kblk, preferred_element_type=f32) * scale \
            - slopes[:, :, None, None] * dist.astype(f32)
        sc = jnp.where(okw, sc, NEG_INF)
        p = jax.nn.softmax(sc, axis=-1)
        return jnp.einsum('bgrqk,bgkd->bgrqd', p.astype(dt), vblk)

    o_win = lax.map(win_block, jnp.arange(T // Q_BLOCK, dtype=jnp.int32))
    o_win = o_win.transpose(1, 2, 3, 0, 4, 5).reshape(B, G, R, T, d)

    gt = jax.nn.sigmoid(gates.astype(f32)).reshape(B, T, G, R, 3).transpose(0, 2, 3, 1, 4)
    o = gt[..., 0:1] * o_cmp + gt[..., 1:2] * o_sel + gt[..., 2:3] * o_win
    o = _rmsnorm(o.transpose(0, 3, 1, 2, 4), norm_g)
    return o.reshape(B, T, NSA_WIDTH).astype(dt)


def _moe(x, w_router, b_router, w_gate_up, b_gate_up, w_down, b_down):
    B, T, D = x.shape
    n_tok = B * T
    xf = x.reshape(n_tok, D)
    logits = jnp.dot(xf, w_router, preferred_element_type=jnp.float32) + b_router.astype(jnp.float32)
    top_logit, top_idx = lax.top_k(logits, TOP_K)
    top_w = jax.nn.softmax(top_logit, axis=-1)
    n_asg = n_tok * TOP_K
    flat_e = top_idx.reshape(n_asg)
    flat_tok = jnp.repeat(jnp.arange(n_tok, dtype=jnp.int32), TOP_K)
    flat_w = top_w.reshape(n_asg)
    order = jnp.argsort(flat_e)
    se, stok, sw = flat_e[order], flat_tok[order], flat_w[order]
    counts = jnp.bincount(flat_e, length=N_EXPERTS)
    start = jnp.cumsum(counts) - counts
    padded = (counts + MOE_BLOCK - 1) // MOE_BLOCK * MOE_BLOCK
    pad_end = jnp.cumsum(padded)
    pad_start = pad_end - padded
    dest = pad_start[se] + jnp.arange(n_asg, dtype=jnp.int32) - start[se]
    n_blocks = -(-n_asg // MOE_BLOCK) + N_EXPERTS
    cap = n_blocks * MOE_BLOCK
    slot_tok = jnp.full((cap,), n_tok, jnp.int32).at[dest].set(stok)
    slot_w = jnp.zeros((cap,), jnp.float32).at[dest].set(sw)
    block_e = jnp.minimum(jnp.searchsorted(pad_end, jnp.arange(n_blocks, dtype=jnp.int32) * MOE_BLOCK, side='right'),
                          N_EXPERTS - 1)
    x_pad = jnp.concatenate([xf, jnp.zeros((1, D), xf.dtype)], axis=0)

    def expert_block(args):
        e, tok, w = args
        hgu = x_pad[tok] @ w_gate_up[e] + b_gate_up[e]
        gate, up = jnp.split(hgu, 2, axis=-1)
        gate = jnp.minimum(gate, SWIGLU_LIMIT)
        up = jnp.clip(up, -SWIGLU_LIMIT, SWIGLU_LIMIT)
        act = gate * jax.nn.sigmoid(SWIGLU_ALPHA * gate) * (up + 1.0)
        y = act @ w_down[e] + b_down[e]
        return y.astype(jnp.float32) * w[:, None]

    y = lax.map(expert_block, (block_e, slot_tok.reshape(n_blocks, MOE_BLOCK), slot_w.reshape(n_blocks, MOE_BLOCK)))
    y = jax.ops.segment_sum(y.reshape(cap, D), slot_tok, num_segments=n_tok + 1)[:n_tok]
    return y.reshape(B, T, D).astype(x.dtype)


def setup_inputs(seed: int = 0) -> dict:
    key = jax.random.key(seed)
    ks = jax.random.split(key, 19)
    f32 = jnp.float32

    def nrm(k, shape, s):
        return jax.random.normal(k, shape, f32) * s

    Lc = CMP_BLOCK * NSA_HEAD_DIM
    return {
        "x": nrm(ks[0], (BATCH, SEQ, D_MODEL), 1.0),
        "norm1_g": 1.0 + nrm(ks[1], (DEPTH, D_MODEL), 0.02),
        "w_in": nrm(ks[2], (DEPTH, D_MODEL, IN_COLS), D_MODEL ** -0.5),
        "hg_lb_logits": nrm(ks[3], (DEPTH + 1, HG_WIDTH), 0.5),
        "hg_norm_g": 1.0 + nrm(ks[4], (DEPTH, HG_DV), 0.02),
        "cmp_pos": nrm(ks[5], (DEPTH, 2, CMP_BLOCK, NSA_HEAD_DIM), 0.1),
        "cmp_w1": nrm(ks[6], (DEPTH, 2, Lc, NSA_HEAD_DIM), Lc ** -0.5),
        "cmp_w2": nrm(ks[7], (DEPTH, 2, NSA_HEAD_DIM, NSA_HEAD_DIM), NSA_HEAD_DIM ** -0.5),
        "nsa_norm_g": 1.0 + nrm(ks[8], (DEPTH, NSA_HEAD_DIM), 0.02),
        "w_out": nrm(ks[9], (DEPTH, D_MIX, D_MODEL), D_MIX ** -0.5),
        "norm2_g": 1.0 + nrm(ks[10], (DEPTH, D_MODEL), 0.02),
        "w_router": nrm(ks[11], (DEPTH, D_MODEL, N_EXPERTS), D_MODEL ** -0.5),
        "b_router": nrm(ks[12], (DEPTH, N_EXPERTS), 0.01),
        "w_gate_up": nrm(ks[13], (DEPTH, N_EXPERTS, D_MODEL, 2 * D_FF), D_MODEL ** -0.5),
        "b_gate_up": nrm(ks[14], (DEPTH, N_EXPERTS, 2 * D_FF), 0.01),
        "w_down": nrm(ks[15], (DEPTH, N_EXPERTS, D_FF, D_MODEL), D_FF ** -0.5),
        "b_down": nrm(ks[16], (DEPTH, N_EXPERTS, D_MODEL), 0.01),
        "final_norm_g": 1.0 + nrm(ks[17], (D_MODEL,), 0.02),
    }


def reference(x, norm1_g, w_in, hg_lb_logits, hg_norm_g, cmp_pos, cmp_w1, cmp_w2, nsa_norm_g, w_out,
              norm2_g, w_router, b_router, w_gate_up, b_gate_up, w_down, b_down, final_norm_g):
    split_points = np.cumsum([HG_WIDTH] * 4 + [NSA_WIDTH] + [KV_WIDTH] * 6).tolist()
    lb_all = jnp.cumsum(jax.nn.softmax(hg_lb_logits.astype(jnp.float32), axis=0), axis=0)
    B, T, _ = x.shape
    h = x
    for l in range(DEPTH):
        u = _rmsnorm(h, norm1_g[l])
        proj = u @ w_in[l]
        (hq, hf, hi, hgt, nq, kcm, vcm, ksl, vsl, kwn, vwn, ngt) = jnp.split(proj, split_points, axis=-1)
        y_hg = _hgrn2(hq, hf, hi, hgt, lb_all[l], hg_norm_g[l])
        y_nsa = _nsa(nq, kcm, vcm, ksl, vsl, kwn, vwn, ngt.reshape(B, T, NSA_HEADS, 3),
                     cmp_pos[l], cmp_w1[l], cmp_w2[l], nsa_norm_g[l])
        h = h + jnp.concatenate([y_hg, y_nsa], axis=-1) @ w_out[l]
        h = h + _moe(_rmsnorm(h, norm2_g[l]), w_router[l], b_router[l], w_gate_up[l], b_gate_up[l],
                     w_down[l], b_down[l])
    return _rmsnorm(h, final_norm_g)
```

```python
import functools

import numpy as np
import jax
import jax.numpy as jnp
from jax import lax
from jax.experimental import pallas as pl
from jax.experimental.pallas import tpu as pltpu

F32 = jnp.float32
BF16 = jnp.bfloat16

D_MODEL = 2048
HG_WIDTH = 1024
HG_DK = 128
HG_HEADS = 8
HG_CHUNK = 64
NSA_WIDTH = 1024
NSA_D = 128
NSA_HEADS = 8
NSA_G = 2
NSA_R = 4
KV_WIDTH = NSA_G * NSA_D
CMP_BLOCK = 32
CMP_STRIDE = 16
SEL_BLOCK = 64
SEL_TOP = 16
WINDOW = 512
N_EXPERTS = 32
TOP_K = 4
D_FF = 2048
SWIGLU_LIMIT = 7.0
SWIGLU_ALPHA = 1.702
EPS = 1e-6
NEG = -1e30
FORCED_SCORE = 1e4
IN_COLS = 4 * HG_WIDTH + NSA_WIDTH + 6 * KV_WIDTH + 3 * NSA_HEADS

LANE = 128
CB_HQ, CB_HF, CB_HI, CB_HG = 0, 8, 16, 24
CB_NQ = 32
CB_KCM, CB_VCM, CB_KSL, CB_VSL, CB_KWN, CB_VWN = 40, 42, 44, 46, 48, 50
CB_GATE = 52

TM_IN = 1024
TN_IN = 768
IN_COLS_PAD = 9 * TN_IN
NORM_ROWS = 32
TQ = 128
KC = 512
WSPAN = WINDOW + TQ
TM_OUT = 512
TM_MOE = 512
FC_MOE = 512
TC_CMB = 128
VMEM_LIMIT = 56 * 1024 * 1024


def _cparams(sem):
    return pltpu.CompilerParams(dimension_semantics=sem, vmem_limit_bytes=VMEM_LIMIT)


def _rms_rows(x, g):
    ms = jnp.mean(x * x, axis=-1, keepdims=True)
    return x * lax.rsqrt(ms + EPS) * g


def _inproj_kernel(x_ref, g_ref, w_ref, o_ref, u_ref):
    @pl.when(pl.program_id(1) == 0)
    def _():
        def body(r, c):
            sl = pl.ds(pl.multiple_of(r * NORM_ROWS, NORM_ROWS), NORM_ROWS)
            u_ref[sl, :] = _rms_rows(x_ref[sl, :], g_ref[...]).astype(BF16)
            return c
        lax.fori_loop(0, TM_IN // NORM_ROWS, body, 0)

    o_ref[...] = jnp.dot(u_ref[...], w_ref[...], preferred_element_type=F32)


def _inproj(x2, g, w_pad):
    n_tok = x2.shape[0]
    return pl.pallas_call(
        _inproj_kernel,
        grid=(n_tok // TM_IN, IN_COLS_PAD // TN_IN),
        in_specs=[
            pl.BlockSpec((TM_IN, D_MODEL), lambda i, j: (i, 0)),
            pl.BlockSpec((1, D_MODEL), lambda i, j: (0, 0)),
            pl.BlockSpec((D_MODEL, TN_IN), lambda i, j: (0, j)),
        ],
        out_specs=pl.BlockSpec((TM_IN, TN_IN), lambda i, j: (i, j)),
        out_shape=jax.ShapeDtypeStruct((n_tok, IN_COLS_PAD), F32),
        scratch_shapes=[pltpu.VMEM((TM_IN, D_MODEL), BF16)],
        compiler_params=_cparams(("parallel", "arbitrary")),
        name="inproj",
    )(x2, g, w_pad)


def _hgrn_kernel(q_ref, f_ref, i_ref, g_ref, lb_ref, ng_ref, o_ref):
    C = HG_CHUNK
    T = q_ref.shape[1]
    lb = lb_ref[...]
    ng = ng_ref[...]
    row = lax.broadcasted_iota(jnp.int32, (C, C), 0)
    col = lax.broadcasted_iota(jnp.int32, (C, C), 1)
    tril = row >= col
    tri_bf = jnp.where(tril, 1.0, 0.0).astype(BF16)

    def body(c, st):
        sl = pl.ds(pl.multiple_of(c * C, C), C)
        f = lb + (1.0 - lb) * jax.nn.sigmoid(f_ref[0, sl, :])
        logf = jnp.log(f)
        k = 1.0 - f
        qf = jax.nn.silu(q_ref[0, sl, :])
        v = i_ref[0, sl, :]
        hi = logf.astype(BF16)
        lo = (logf - hi.astype(F32)).astype(BF16)
        b = jnp.dot(tri_bf, hi, preferred_element_type=F32) + jnp.dot(tri_bf, lo, preferred_element_type=F32)
        b_last = b[C - 1:C, :]
        q_dec = (qf * jnp.exp(b)).astype(BF16)
        k_dec = (k * jnp.exp(-b)).astype(BF16)
        v_bf = v.astype(BF16)
        a = lax.dot_general(q_dec, k_dec, (((1,), (1,)), ((), ())), preferred_element_type=F32)
        a = jnp.where(tril, a, 0.0).astype(BF16)
        o = jnp.dot(a, v_bf, preferred_element_type=F32)
        o = o + lax.dot_general(q_dec, st.astype(BF16), (((1,), (1,)), ((), ())), preferred_element_type=F32)
        k_rem = (k * jnp.exp(b_last - b)).astype(BF16)
        st_new = st * jnp.exp(b_last) + lax.dot_general(v_bf, k_rem, (((0,), (0,)), ((), ())),
                                                        preferred_element_type=F32)
        y = _rms_rows(o, ng) * jax.nn.silu(g_ref[0, sl, :])
        o_ref[0, sl, :] = y.astype(o_ref.dtype)
        return st_new

    lax.fori_loop(0, T // C, body, jnp.zeros((HG_DK, HG_DK), F32))


def _hgrn(proj3, lb, ng):
    B, T, _ = proj3.shape

    def col(cb):
        return pl.BlockSpec((1, T, LANE), lambda b, h, cb=cb: (b, 0, cb + h))

    return pl.pallas_call(
        _hgrn_kernel,
        grid=(B, HG_HEADS),
        in_specs=[col(CB_HQ), col(CB_HF), col(CB_HI), col(CB_HG),
                  pl.BlockSpec((1, LANE), lambda b, h: (0, h)),
                  pl.BlockSpec((1, LANE), lambda b, h: (0, 0))],
        out_specs=pl.BlockSpec((1, T, LANE), lambda b, h: (b, 0, h)),
        out_shape=jax.ShapeDtypeStruct((B, T, HG_WIDTH), BF16),
        compiler_params=_cparams(("parallel", "parallel")),
        name="hgrn2",
    )(proj3, proj3, proj3, proj3, lb, ng)


def _gelu_tanh(x):
    return 0.5 * x * (1.0 + jnp.tanh(0.7978845608028654 * (x + 0.044715 * (x * x * x))))


def _compress_kernel(x_ref, pos_ref, w1_ref, w2_ref, o_ref):
    T = x_ref.shape[1]
    ng = T // CMP_STRIDE
    acc_a = jnp.zeros((ng, NSA_D), F32)
    acc_b = jnp.zeros((ng, NSA_D), F32)
    for r in range(CMP_STRIDE):
        xr = x_ref[0, pl.ds(r, ng, stride=CMP_STRIDE), :]
        xa = (xr + pos_ref[0, r:r + 1, :]).astype(BF16)
        xb = (xr + pos_ref[0, CMP_STRIDE + r:CMP_STRIDE + r + 1, :]).astype(BF16)
        acc_a = acc_a + jnp.dot(xa, w1_ref[0, r], preferred_element_type=F32)
        acc_b = acc_b + jnp.dot(xb, w1_ref[0, CMP_STRIDE + r], preferred_element_type=F32)
    hdn = _gelu_tanh(acc_a + pltpu.roll(acc_b, ng - 1, 0))
    o_ref[0, 0, 0] = jnp.dot(hdn.astype(BF16), w2_ref[0], preferred_element_type=F32)


def _compress(proj3, pos, w1r, w2b):
    B, T, _ = proj3.shape
    ng = T // CMP_STRIDE
    return pl.pallas_call(
        _compress_kernel,
        grid=(B, NSA_G, 2),
        in_specs=[
            pl.BlockSpec((1, T, LANE), lambda b, g, j: (b, 0, CB_KCM + 2 * j + g)),
            pl.BlockSpec((1, CMP_BLOCK, NSA_D), lambda b, g, j: (j, 0, 0)),
            pl.BlockSpec((1, CMP_BLOCK, NSA_D, NSA_D), lambda b, g, j: (j, 0, 0, 0)),
            pl.BlockSpec((1, NSA_D, NSA_D), lambda b, g, j: (j, 0, 0)),
        ],
        out_specs=pl.BlockSpec((1, 1, 1, ng, NSA_D), lambda b, g, j: (b, g, j, 0, 0)),
        out_shape=jax.ShapeDtypeStruct((B, NSA_G, 2, ng, NSA_D), F32),
        compiler_params=_cparams(("parallel", "parallel", "parallel")),
        name="nsa_compress",
    )(proj3, pos, w1r, w2b)


def _overlap_matrix(T):
    n_cmp = (T - CMP_BLOCK) // CMP_STRIDE + 1
    n_sel = T // SEL_BLOCK
    ratio_s = SEL_BLOCK // CMP_STRIDE
    ratio_c = CMP_BLOCK // CMP_STRIDE
    jj, mm, nn = np.meshgrid(np.arange(n_sel), np.arange(ratio_s), np.arange(ratio_c), indexing='ij')
    ii = ratio_s * jj + mm + nn - (ratio_c - 1)
    ok = (ii >= 0) & (ii < n_cmp)
    overlap = np.zeros((T // CMP_STRIDE, LANE), np.float32)
    np.add.at(overlap, (ii[ok], jj[ok]), 1.0)
    return overlap


def _expand_matrix(T):
    e = np.zeros((LANE, T), np.float32)
    e[np.arange(T) // SEL_BLOCK, np.arange(T)] = 1.0
    return e


def _nsa_kernel(slopes_ref, q_ref, ks_ref, vs_ref, kw_ref, vw_ref, gt_ref, kvc_ref, ov_ref, e_ref, ng_ref,
                o_ref, ksb, vsb, kwb, vwb, selk_ref):
    T = ks_ref.shape[1]
    R, D = NSA_R, NSA_D
    N = R * TQ
    n_sel = T // SEL_BLOCK
    top_n = min(SEL_TOP, n_sel)
    g = pl.program_id(1)
    qi = pl.program_id(2)
    q0 = qi * TQ

    @pl.when(qi == 0)
    def _():
        cr = 256

        def body(r, c):
            sl = pl.ds(pl.multiple_of(r * cr, cr), cr)
            ksb[sl, :] = ks_ref[0, sl, :].astype(BF16)
            vsb[sl, :] = vs_ref[0, sl, :].astype(BF16)
            kwb[sl, :] = kw_ref[0, sl, :].astype(BF16)
            vwb[sl, :] = vw_ref[0, sl, :].astype(BF16)
            return c
        lax.fori_loop(0, T // cr, body, 0)

    def stack(x):
        return jnp.concatenate([x] * R, axis=0)

    q = q_ref[0] * (D ** -0.5)
    qs = jnp.concatenate([q[:, r * D:(r + 1) * D] for r in range(R)], axis=0).astype(BF16)
    slope = jnp.concatenate([jnp.full((TQ, 1), slopes_ref[g, r], F32) for r in range(R)], axis=0)
    t1 = q0 + lax.broadcasted_iota(jnp.int32, (TQ, 1), 0)
    trow = stack(t1)

    kc = kvc_ref[0, 0, 0].astype(BF16)
    vc = kvc_ref[0, 0, 1].astype(BF16)
    ncmp = kc.shape[0]
    cend = lax.broadcasted_iota(jnp.int32, (1, ncmp), 1) * CMP_STRIDE + (CMP_BLOCK - 1)
    cend = jnp.where(cend < T, cend, 4 * T)
    valid_c = trow >= cend
    s = lax.dot_general(qs, kc, (((1,), (1,)), ((), ())), preferred_element_type=F32)
    s = s + slope * cend.astype(F32)
    s = jnp.where(valid_c, s, NEG)
    mx = jnp.max(s, axis=-1, keepdims=True)
    e = jnp.where(valid_c, jnp.exp(s - mx), 0.0)
    l = jnp.sum(e, axis=-1, keepdims=True)
    p_cmp = e / jnp.where(l > 0.0, l, 1.0)
    o_cmp = jnp.dot(p_cmp.astype(BF16), vc, preferred_element_type=F32)

    psum = p_cmp[0:TQ]
    for r in range(1, R):
        psum = psum + p_cmp[r * TQ:(r + 1) * TQ]
    ph = psum.astype(BF16)
    plo = (psum - ph.astype(F32)).astype(BF16)
    ov = ov_ref[...]
    imp = jnp.dot(ph, ov, preferred_element_type=F32) + jnp.dot(plo, ov, preferred_element_type=F32)

    blk = lax.broadcasted_iota(jnp.int32, (TQ, LANE), 1)
    cur = t1 // SEL_BLOCK
    forced = (blk == 0) | (blk == cur) | (blk == cur - 1)
    score = jnp.where(forced, FORCED_SCORE, jnp.where(blk <= cur, imp, -FORCED_SCORE))
    score = jnp.where(blk < n_sel, score, -4.0 * FORCED_SCORE)
    rank = jnp.zeros((TQ, LANE), F32)
    for kk in range(n_sel):
        sk = score[:, kk:kk + 1]
        beats = jnp.where(sk > score, 1.0, jnp.where(sk == score, jnp.where(blk > kk, 1.0, 0.0), 0.0))
        rank = rank + beats
    selmask = jnp.where(rank < float(top_n), 1.0, 0.0).astype(BF16)
    for c in range(T // KC):
        selk_ref[c] = jnp.dot(selmask, e_ref[:, c * KC:(c + 1) * KC], preferred_element_type=F32)

    rloc = lax.broadcasted_iota(jnp.int32, (N, KC), 0) % TQ
    rel = rloc - lax.broadcasted_iota(jnp.int32, (N, KC), 1)
    srel = slope * rel.astype(F32)

    def sel_body(c, carry):
        m, lsum, acc = carry
        k0 = pl.multiple_of(c * KC, KC)
        kch = ksb[pl.ds(k0, KC), :]
        vch = vsb[pl.ds(k0, KC), :]
        s1 = lax.dot_general(qs, kch, (((1,), (1,)), ((), ())), preferred_element_type=F32) - srel
        selm = stack(selk_ref[c])
        valid = jnp.where(rel >= k0 - q0, selm, 0.0) > 0.5
        cc = slope * (k0 - q0).astype(F32)
        sm = jnp.where(valid, s1, NEG)
        m_new = jnp.maximum(m, jnp.max(sm, axis=-1, keepdims=True) + cc)
        p = jnp.where(valid, jnp.exp(sm - (m_new - cc)), 0.0)
        alpha = jnp.exp(m - m_new)
        lsum = alpha * lsum + jnp.sum(p, axis=-1, keepdims=True)
        acc = alpha * acc + jnp.dot(p.astype(BF16), vch, preferred_element_type=F32)
        return m_new, lsum, acc

    n_ch = (q0 + TQ - 1) // KC + 1
    m0 = jnp.full((N, 1), NEG, F32)
    _, l_sel, acc_sel = lax.fori_loop(0, n_ch, sel_body, (m0, jnp.zeros((N, 1), F32), jnp.zeros((N, D), F32)))
    o_sel = acc_sel / l_sel

    start = pl.multiple_of(jnp.maximum(q0 - WINDOW, 0), TQ)
    kw = kwb[pl.ds(start, WSPAN), :]
    vw = vwb[pl.ds(start, WSPAN), :]
    relw = (lax.broadcasted_iota(jnp.int32, (N, WSPAN), 0) % TQ) - lax.broadcasted_iota(jnp.int32, (N, WSPAN), 1)
    dist = relw + (q0 - start)
    sw = lax.dot_general(qs, kw, (((1,), (1,)), ((), ())), preferred_element_type=F32) - slope * relw.astype(F32)
    valid_w = jnp.where(dist >= 0, jnp.where(dist < WINDOW, 1.0, 0.0), 0.0) > 0.5
    sw = jnp.where(valid_w, sw, NEG)
    mw = jnp.max(sw, axis=-1, keepdims=True)
    pw = jnp.where(valid_w, jnp.exp(sw - mw), 0.0)
    lw = jnp.sum(pw, axis=-1, keepdims=True)
    o_win = jnp.dot(pw.astype(BF16), vw, preferred_element_type=F32) / lw

    sig = jax.nn.sigmoid(gt_ref[0])
    lane = lax.broadcasted_iota(jnp.int32, (TQ, LANE), 1)

    def gate_col(br):
        cols = []
        for r in range(R):
            cidx = (g * R + r) * 3 + br
            cols.append(jnp.sum(jnp.where(lane == cidx, sig, 0.0), axis=-1, keepdims=True))
        return jnp.concatenate(cols, axis=0)

    o = gate_col(0) * o_cmp + gate_col(1) * o_sel + gate_col(2) * o_win
    y = _rms_rows(o, ng_ref[...])
    for r in range(R):
        o_ref[0, :, r * D:(r + 1) * D] = y[r * TQ:(r + 1) * TQ].astype(o_ref.dtype)


def _nsa(proj3, kvc, slopes, ov, ex, ng):
    B, T, _ = proj3.shape
    ncmp = T // CMP_STRIDE

    def kv(cb):
        return pl.BlockSpec((1, T, LANE), lambda b, g, qi, s, cb=cb: (b, 0, cb + g))

    grid_spec = pltpu.PrefetchScalarGridSpec(
        num_scalar_prefetch=1,
        grid=(B, NSA_G, T // TQ),
        in_specs=[
            pl.BlockSpec((1, TQ, NSA_R * NSA_D), lambda b, g, qi, s: (b, qi, CB_NQ // NSA_R + g)),
            kv(CB_KSL), kv(CB_VSL), kv(CB_KWN), kv(CB_VWN),
            pl.BlockSpec((1, TQ, LANE), lambda b, g, qi, s: (b, qi, CB_GATE)),
            pl.BlockSpec((1, 1, 2, ncmp, NSA_D), lambda b, g, qi, s: (b, g, 0, 0, 0)),
            pl.BlockSpec((ncmp, LANE), lambda b, g, qi, s: (0, 0)),
            pl.BlockSpec((LANE, T), lambda b, g, qi, s: (0, 0)),
            pl.BlockSpec((1, NSA_D), lambda b, g, qi, s: (0, 0)),
        ],
        out_specs=pl.BlockSpec((1, TQ, NSA_R * NSA_D), lambda b, g, qi, s: (b, qi, g)),
        scratch_shapes=[pltpu.VMEM((T, NSA_D), BF16)] * 4 + [pltpu.VMEM((T // KC, TQ, KC), F32)],
    )
    return pl.pallas_call(
        _nsa_kernel,
        grid_spec=grid_spec,
        out_shape=jax.ShapeDtypeStruct((B, T, NSA_WIDTH), BF16),
        compiler_params=_cparams(("parallel", "parallel", "arbitrary")),
        name="nsa_attention",
    )(slopes, proj3, proj3, proj3, proj3, proj3, proj3, kvc, ov, ex, ng)


def _outproj_kernel(hg_ref, ns_ref, x_ref, wo_ref, g2_ref, wr_ref, br_ref,
                    h1_ref, xn_ref, idx_ref, w_ref):
    h1_ref[...] = (x_ref[...]
                   + jnp.dot(hg_ref[...], wo_ref[0:HG_WIDTH, :], preferred_element_type=F32)
                   + jnp.dot(ns_ref[...], wo_ref[HG_WIDTH:, :], preferred_element_type=F32))

    def body(r, c):
        sl = pl.ds(pl.multiple_of(r * NORM_ROWS, NORM_ROWS), NORM_ROWS)
        xn_ref[sl, :] = _rms_rows(h1_ref[sl, :], g2_ref[...])
        return c
    lax.fori_loop(0, TM_OUT // NORM_ROWS, body, 0)

    logits = jnp.dot(xn_ref[...], wr_ref[...], preferred_element_type=F32) + br_ref[...]
    lane = lax.broadcasted_iota(jnp.int32, logits.shape, 1)
    lg = jnp.where(lane < N_EXPERTS, logits, -jnp.inf)
    vals, idxs = [], []
    for _ in range(TOP_K):
        mx = jnp.max(lg, axis=-1, keepdims=True)
        ix = jnp.min(jnp.where(lg == mx, lane, LANE), axis=-1, keepdims=True)
        vals.append(mx)
        idxs.append(ix)
        lg = jnp.where(lane == ix, -jnp.inf, lg)
    es = [jnp.exp(v - vals[0]) for v in vals]
    den = es[0]
    for e in es[1:]:
        den = den + e
    idx_out = jnp.zeros(logits.shape, jnp.int32)
    w_out = jnp.zeros(logits.shape, F32)
    for k in range(TOP_K):
        idx_out = jnp.where(lane == k, idxs[k], idx_out)
        w_out = jnp.where(lane == k, es[k] / den, w_out)
    idx_ref[...] = idx_out
    w_ref[...] = w_out


def _outproj(y_hg, y_nsa, x2, wo, g2, wr_pad, br_pad):
    n_tok = x2.shape[0]
    row = lambda w: pl.BlockSpec((TM_OUT, w), lambda i: (i, 0))
    full = lambda a, b: pl.BlockSpec((a, b), lambda i: (0, 0))
    return pl.pallas_call(
        _outproj_kernel,
        grid=(n_tok // TM_OUT,),
        in_specs=[row(HG_WIDTH), row(NSA_WIDTH), row(D_MODEL), full(D_MODEL, D_MODEL), full(1, D_MODEL),
                  full(D_MODEL, LANE), full(1, LANE)],
        out_specs=[row(D_MODEL), row(D_MODEL), row(LANE), row(LANE)],
        out_shape=[jax.ShapeDtypeStruct((n_tok, D_MODEL), F32), jax.ShapeDtypeStruct((n_tok, D_MODEL), F32),
                   jax.ShapeDtypeStruct((n_tok, LANE), jnp.int32), jax.ShapeDtypeStruct((n_tok, LANE), F32)],
        compiler_params=_cparams(("parallel",)),
        name="outproj_router",
    )(y_hg, y_nsa, x2, wo, g2, wr_pad, br_pad)


def _moe_kernel(be_ref, nu_ref, tok_ref, xn_hbm, wg_ref, wu_ref, wd_ref, bg_ref, bu_ref, bd_ref,
                y_ref, xbuf, xg, sem):
    i = pl.program_id(0)
    f = pl.program_id(1)
    slot = i % 2
    nu = nu_ref[0]

    def row_copy(tok, sl, r):
        return pltpu.make_async_copy(xn_hbm.at[tok], xbuf.at[sl, r], sem.at[sl])

    def issue(blk, sl):
        def body(r, c):
            row_copy(tok_ref[blk * TM_MOE + r], sl, r).start()
            return c
        lax.fori_loop(0, TM_MOE, body, 0)

    def wait_all(sl):
        def body(r, c):
            row_copy(0, sl, r).wait()
            return c
        lax.fori_loop(0, TM_MOE, body, 0)

    @pl.when(f == 0)
    def _():
        @pl.when(i == 0)
        def _():
            issue(0, 0)

        @pl.when(i < nu)
        def _():
            wait_all(slot)

            def body(r, c):
                sl = pl.ds(pl.multiple_of(r * 64, 64), 64)
                xg[sl, :] = xbuf[slot, sl, :].astype(BF16)
                return c
            lax.fori_loop(0, TM_MOE // 64, body, 0)

        @pl.when(i + 1 < nu)
        def _():
            issue(i + 1, 1 - slot)

    @pl.when(i < nu)
    def _():
        x = xg[...]
        gate = jnp.dot(x, wg_ref[0], preferred_element_type=F32) + bg_ref[0]
        up = jnp.dot(x, wu_ref[0], preferred_element_type=F32) + bu_ref[0]
        gate = jnp.minimum(gate, SWIGLU_LIMIT)
        up = jnp.clip(up, -SWIGLU_LIMIT, SWIGLU_LIMIT)
        act = gate * jax.nn.sigmoid(SWIGLU_ALPHA * gate) * (up + 1.0)
        part = jnp.dot(act.astype(BF16), wd_ref[0], preferred_element_type=F32)

        @pl.when(f == 0)
        def _():
            y_ref[...] = part + bd_ref[0]

        @pl.when(f > 0)
        def _():
            y_ref[...] += part

    @pl.when((i >= nu) & (f == 0))
    def _():
        y_ref[...] = jnp.zeros_like(y_ref)


def _moe(block_e, n_used, slot_tok, xn, wgu, wd, bgu, bd):
    n_blocks = block_e.shape[0]
    nf = D_FF // FC_MOE

    def eff(i, f, be, nu):
        live = i < nu[0]
        ii = jnp.where(live, i, nu[0] - 1)
        return be[ii], jnp.where(live, f, nf - 1)

    def wg_map(i, f, be, nu, tk):
        e, ff = eff(i, f, be, nu)
        return (e, 0, ff)

    def wu_map(i, f, be, nu, tk):
        e, ff = eff(i, f, be, nu)
        return (e, 0, nf + ff)

    def wd_map(i, f, be, nu, tk):
        e, ff = eff(i, f, be, nu)
        return (e, ff, 0)

    def bd_map(i, f, be, nu, tk):
        e, ff = eff(i, f, be, nu)
        return (e, 0, 0)

    grid_spec = pltpu.PrefetchScalarGridSpec(
        num_scalar_prefetch=3,
        grid=(n_blocks, nf),
        in_specs=[
            pl.BlockSpec(memory_space=pl.ANY),
            pl.BlockSpec((1, D_MODEL, FC_MOE), wg_map),
            pl.BlockSpec((1, D_MODEL, FC_MOE), wu_map),
            pl.BlockSpec((1, FC_MOE, D_MODEL), wd_map),
            pl.BlockSpec((1, 1, FC_MOE), wg_map),
            pl.BlockSpec((1, 1, FC_MOE), wu_map),
            pl.BlockSpec((1, 1, D_MODEL), bd_map),
        ],
        out_specs=pl.BlockSpec((TM_MOE, D_MODEL), lambda i, f, be, nu, tk: (i, 0)),
        scratch_shapes=[pltpu.VMEM((2, TM_MOE, D_MODEL), F32), pltpu.VMEM((TM_MOE, D_MODEL), BF16),
                        pltpu.SemaphoreType.DMA((2,))],
    )
    return pl.pallas_call(
        _moe_kernel,
        grid_spec=grid_spec,
        out_shape=jax.ShapeDtypeStruct((n_blocks * TM_MOE, D_MODEL), F32),
        compiler_params=_cparams(("arbitrary", "arbitrary")),
        name="moe_experts",
    )(block_e, n_used, slot_tok, xn, wgu, wgu, wd, bgu, bgu, bd)


def _combine_kernel(pos_ref, y_hbm, h1_ref, w_ref, g_ref, o_ref, ybuf, sem):
    i = pl.program_id(0)
    n = pl.num_programs(0)
    slot = i % 2

    def row_copy(src, sl, k, r):
        return pltpu.make_async_copy(y_hbm.at[src], ybuf.at[sl, k, r], sem.at[sl])

    def issue(blk, sl):
        def body(r, c):
            for k in range(TOP_K):
                row_copy(pos_ref[(blk * TC_CMB + r) * TOP_K + k], sl, k, r).start()
            return c
        lax.fori_loop(0, TC_CMB, body, 0)

    def wait_all(sl):
        def body(r, c):
            for k in range(TOP_K):
                row_copy(0, sl, k, r).wait()
            return c
        lax.fori_loop(0, TC_CMB, body, 0)

    @pl.when(i == 0)
    def _():
        issue(0, 0)

    wait_all(slot)

    @pl.when(i + 1 < n)
    def _():
        issue(i + 1, 1 - slot)

    w = w_ref[...]
    h = h1_ref[...]
    for k in range(TOP_K):
        h = h + w[:, k:k + 1] * ybuf[slot, k]
    o_ref[...] = _rms_rows(h, g_ref[...])


def _combine(pos, y_sorted, h1, top_w, g):
    n_tok = h1.shape[0]
    grid_spec = pltpu.PrefetchScalarGridSpec(
        num_scalar_prefetch=1,
        grid=(n_tok // TC_CMB,),
        in_specs=[
            pl.BlockSpec(memory_space=pl.ANY),
            pl.BlockSpec((TC_CMB, D_MODEL), lambda i, p: (i, 0)),
            pl.BlockSpec((TC_CMB, LANE), lambda i, p: (i, 0)),
            pl.BlockSpec((1, D_MODEL), lambda i, p: (0, 0)),
        ],
        out_specs=pl.BlockSpec((TC_CMB, D_MODEL), lambda i, p: (i, 0)),
        scratch_shapes=[pltpu.VMEM((2, TOP_K, TC_CMB, D_MODEL), F32), pltpu.SemaphoreType.DMA((2,))],
    )
    return pl.pallas_call(
        _combine_kernel,
        grid_spec=grid_spec,
        out_shape=jax.ShapeDtypeStruct((n_tok, D_MODEL), F32),
        compiler_params=_cparams(("arbitrary",)),
        name="moe_combine",
    )(pos, y_sorted, h1, top_w, g)


def _dispatch(top_idx):
    n_tok = top_idx.shape[0]
    n_asg = n_tok * TOP_K
    flat_e = top_idx.reshape(n_asg)
    flat_tok = jnp.repeat(jnp.arange(n_tok, dtype=jnp.int32), TOP_K)
    order = jnp.argsort(flat_e)
    se, stok = flat_e[order], flat_tok[order]
    counts = jnp.bincount(flat_e, length=N_EXPERTS).astype(jnp.int32)
    start = jnp.cumsum(counts) - counts
    padded = (counts + TM_MOE - 1) // TM_MOE * TM_MOE
    pad_end = jnp.cumsum(padded)
    pad_start = pad_end - padded
    dest = (pad_start[se] + jnp.arange(n_asg, dtype=jnp.int32) - start[se]).astype(jnp.int32)
    n_blocks = n_asg // TM_MOE + N_EXPERTS
    cap = n_blocks * TM_MOE
    slot_tok = jnp.zeros((cap,), jnp.int32).at[dest].set(stok)
    pos = jnp.zeros((n_asg,), jnp.int32).at[order].set(dest)
    block_e = jnp.minimum(
        jnp.searchsorted(pad_end, jnp.arange(n_blocks, dtype=jnp.int32) * TM_MOE, side='right'),
        N_EXPERTS - 1).astype(jnp.int32)
    n_used = (pad_end[-1:] // TM_MOE).astype(jnp.int32)
    return block_e, n_used, slot_tok, pos


def _alibi_slopes():
    h = np.arange(1, NSA_HEADS + 1, dtype=np.float32)
    return np.power(np.float32(2.0), -8.0 * h / NSA_HEADS).astype(np.float32).reshape(NSA_G, NSA_R)


def kernel(x, norm1_g, w_in, hg_lb_logits, hg_norm_g, cmp_pos, cmp_w1, cmp_w2, nsa_norm_g, w_out,
           norm2_g, w_router, b_router, w_gate_up, b_gate_up, w_down, b_down, final_norm_g):
    B, T, D = x.shape
    n_tok = B * T
    l = 0
    x2 = x.reshape(n_tok, D)

    w_in_pad = jnp.pad(w_in[l], ((0, 0), (0, IN_COLS_PAD - IN_COLS))).astype(BF16)
    lb_all = jnp.cumsum(jax.nn.softmax(hg_lb_logits.astype(F32), axis=0), axis=0)
    lb = lb_all[l].reshape(1, HG_WIDTH)
    w1r = cmp_w1[l].reshape(2, CMP_BLOCK, NSA_D, NSA_D).astype(BF16)
    w2b = cmp_w2[l].astype(BF16)
    wo = w_out[l].astype(BF16)
    wr_pad = jnp.pad(w_router[l], ((0, 0), (0, LANE - N_EXPERTS)))
    br_pad = jnp.pad(b_router[l], (0, LANE - N_EXPERTS)).reshape(1, LANE)
    wgu = w_gate_up[l].astype(BF16)
    wd = w_down[l].astype(BF16)
    bgu = b_gate_up[l].reshape(N_EXPERTS, 1, 2 * D_FF)
    bd = b_down[l].reshape(N_EXPERTS, 1, D_MODEL)

    proj = _inproj(x2, norm1_g[l].reshape(1, D), w_in_pad)
    proj3 = proj.reshape(B, T, IN_COLS_PAD)
    y_hg = _hgrn(proj3, lb, hg_norm_g[l].reshape(1, HG_DK))
    kvc = _compress(proj3, cmp_pos[l], w1r, w2b)
    y_nsa = _nsa(proj3, kvc, jnp.asarray(_alibi_slopes()), jnp.asarray(_overlap_matrix(T), BF16),
                 jnp.asarray(_expand_matrix(T), BF16), nsa_norm_g[l].reshape(1, NSA_D))
    h1, xn, top_idx, top_w = _outproj(y_hg.reshape(n_tok, HG_WIDTH), y_nsa.reshape(n_tok, NSA_WIDTH), x2, wo,
                                      norm2_g[l].reshape(1, D), wr_pad, br_pad)
    block_e, n_used, slot_tok, pos = _dispatch(top_idx[:, :TOP_K])
    y_sorted = _moe(block_e, n_used, slot_tok, xn, wgu, wd, bgu, bd)
    out = _combine(pos, y_sorted, h1, top_w, final_norm_g.reshape(1, D))
    return out.reshape(B, T, D)
```

```python
import functools

import numpy as np
import jax
import jax.numpy as jnp
from jax import lax
from jax.experimental import pallas as pl
from jax.experimental.pallas import tpu as pltpu

F32 = jnp.float32
BF16 = jnp.bfloat16

D_MODEL = 2048
HG_WIDTH = 1024
HG_DK = 128
HG_HEADS = 8
HG_CHUNK = 64
NSA_WIDTH = 1024
NSA_D = 128
NSA_HEADS = 8
NSA_G = 2
NSA_R = 4
KV_WIDTH = NSA_G * NSA_D
CMP_BLOCK = 32
CMP_STRIDE = 16
SEL_BLOCK = 64
SEL_TOP = 16
WINDOW = 512
N_EXPERTS = 32
TOP_K = 4
D_FF = 2048
SWIGLU_LIMIT = 7.0
SWIGLU_ALPHA = 1.702
EPS = 1e-6
NEG = -1e30
FORCED_SCORE = 1e4
IN_COLS = 4 * HG_WIDTH + NSA_WIDTH + 6 * KV_WIDTH + 3 * NSA_HEADS

LANE = 128
CB_HQ, CB_HF, CB_HI, CB_HG = 0, 8, 16, 24
CB_NQ = 32
CB_KCM, CB_VCM, CB_KSL, CB_VSL, CB_KWN, CB_VWN = 40, 42, 44, 46, 48, 50
CB_GATE = 52

TM_IN = 1024
TN_IN = 768
IN_COLS_PAD = 9 * TN_IN
NORM_ROWS = 32
HG_TT = 256
TQ = 128
KC = 512
WSPAN = WINDOW + TQ
TM_OUT = 512
SEG = D_MODEL // LANE
SB_MOE = 1280
TS_MOE = 256
FC_MOE = 256
NF_MOE = D_FF // FC_MOE
NN_MOE = D_MODEL // FC_MOE
TC_CMB = 128
VMEM_LIMIT = 56 * 1024 * 1024


def _cparams(sem):
    return pltpu.CompilerParams(dimension_semantics=sem, vmem_limit_bytes=VMEM_LIMIT)


def _rms_rows(x, g):
    ms = jnp.mean(x * x, axis=-1, keepdims=True)
    return x * lax.rsqrt(ms + EPS) * g


def _inproj_kernel(x_ref, g_ref, w_ref, o_ref, u_ref):
    @pl.when(pl.program_id(1) == 0)
    def _():
        def body(r, c):
            sl = pl.ds(pl.multiple_of(r * NORM_ROWS, NORM_ROWS), NORM_ROWS)
            u_ref[sl, :] = _rms_rows(x_ref[sl, :], g_ref[...]).astype(BF16)
            return c
        lax.fori_loop(0, TM_IN // NORM_ROWS, body, 0)

    o_ref[...] = jnp.dot(u_ref[...], w_ref[...], preferred_element_type=F32)


def _inproj(x2, g, w_pad):
    n_tok = x2.shape[0]
    return pl.pallas_call(
        _inproj_kernel,
        grid=(n_tok // TM_IN, IN_COLS_PAD // TN_IN),
        in_specs=[
            pl.BlockSpec((TM_IN, D_MODEL), lambda i, j: (i, 0)),
            pl.BlockSpec((1, D_MODEL), lambda i, j: (0, 0)),
            pl.BlockSpec((D_MODEL, TN_IN), lambda i, j: (0, j)),
        ],
        out_specs=pl.BlockSpec((TM_IN, TN_IN), lambda i, j: (i, j)),
        out_shape=jax.ShapeDtypeStruct((n_tok, IN_COLS_PAD), F32),
        scratch_shapes=[pltpu.VMEM((TM_IN, D_MODEL), BF16)],
        compiler_params=_cparams(("parallel", "arbitrary")),
        name="inproj",
    )(x2, g, w_pad)


def _hgrn_kernel(q_ref, f_ref, i_ref, g_ref, lb_ref, ng_ref, o_ref, st_ref):
    C = HG_CHUNK
    ng = ng_ref[...]
    row = lax.broadcasted_iota(jnp.int32, (C, C), 0)
    col = lax.broadcasted_iota(jnp.int32, (C, C), 1)
    tril = row >= col
    tri_bf = jnp.where(tril, 1.0, 0.0).astype(BF16)

    @pl.when(pl.program_id(1) == 0)
    def _():
        st_ref[...] = jnp.zeros_like(st_ref)

    def body(c, carry):
        sl = pl.ds(pl.multiple_of(c * C, C), C)
        for h in range(HG_HEADS):
            hs = slice(h * HG_DK, (h + 1) * HG_DK)
            lb = lb_ref[:, hs]
            st = st_ref[h]
            f = lb + (1.0 - lb) * jax.nn.sigmoid(f_ref[0, sl, hs])
            logf = jnp.log(f)
            k = 1.0 - f
            qf = jax.nn.silu(q_ref[0, sl, hs])
            v_bf = i_ref[0, sl, hs].astype(BF16)
            hi = logf.astype(BF16)
            lo = (logf - hi.astype(F32)).astype(BF16)
            b = jnp.dot(tri_bf, hi, preferred_element_type=F32) + jnp.dot(tri_bf, lo, preferred_element_type=F32)
            b_last = b[C - 1:C, :]
            q_dec = (qf * jnp.exp(b)).astype(BF16)
            k_dec = (k * jnp.exp(-b)).astype(BF16)
            a = lax.dot_general(q_dec, k_dec, (((1,), (1,)), ((), ())), preferred_element_type=F32)
            a = jnp.where(tril, a, 0.0).astype(BF16)
            o = jnp.dot(a, v_bf, preferred_element_type=F32)
            o = o + lax.dot_general(q_dec, st.astype(BF16), (((1,), (1,)), ((), ())), preferred_element_type=F32)
            k_rem = (k * jnp.exp(b_last - b)).astype(BF16)
            st_ref[h] = st * jnp.exp(b_last) + lax.dot_general(v_bf, k_rem, (((0,), (0,)), ((), ())),
                                                                 preferred_element_type=F32)
            y = _rms_rows(o, ng) * jax.nn.silu(g_ref[0, sl, hs])
            o_ref[0, sl, hs] = y.astype(o_ref.dtype)
        return carry

    lax.fori_loop(0, HG_TT // C, body, 0)


def _hgrn(proj3, lb, ng):
    B, T, _ = proj3.shape

    def col(cb):
        return pl.BlockSpec((1, HG_TT, HG_WIDTH), lambda b, t, cb=cb: (b, t, cb // HG_HEADS))

    return pl.pallas_call(
        _hgrn_kernel,
        grid=(B, T // HG_TT),
        in_specs=[col(CB_HQ), col(CB_HF), col(CB_HI), col(CB_HG),
                  pl.BlockSpec((1, HG_WIDTH), lambda b, t: (0, 0)),
                  pl.BlockSpec((1, LANE), lambda b, t: (0, 0))],
        out_specs=pl.BlockSpec((1, HG_TT, HG_WIDTH), lambda b, t: (b, t, 0)),
        out_shape=jax.ShapeDtypeStruct((B, T, HG_WIDTH), BF16),
        scratch_shapes=[pltpu.VMEM((HG_HEADS, HG_DK, HG_DK), F32)],
        compiler_params=_cparams(("parallel", "arbitrary")),
        name="hgrn2",
    )(proj3, proj3, proj3, proj3, lb, ng)


def _gelu_tanh(x):
    return 0.5 * x * (1.0 + jnp.tanh(0.7978845608028654 * (x + 0.044715 * (x * x * x))))


def _compress_kernel(x_ref, pos_ref, w1_ref, w2_ref, o_ref):
    T = x_ref.shape[1]
    ng = T // CMP_STRIDE
    acc_a = jnp.zeros((ng, NSA_D), F32)
    acc_b = jnp.zeros((ng, NSA_D), F32)
    for r in range(CMP_STRIDE):
        xr = x_ref[0, pl.ds(r, ng, stride=CMP_STRIDE), :]
        xa = (xr + pos_ref[0, r:r + 1, :]).astype(BF16)
        xb = (xr + pos_ref[0, CMP_STRIDE + r:CMP_STRIDE + r + 1, :]).astype(BF16)
        acc_a = acc_a + jnp.dot(xa, w1_ref[0, r], preferred_element_type=F32)
        acc_b = acc_b + jnp.dot(xb, w1_ref[0, CMP_STRIDE + r], preferred_element_type=F32)
    hdn = _gelu_tanh(acc_a + pltpu.roll(acc_b, ng - 1, 0))
    o_ref[0, 0, 0] = jnp.dot(hdn.astype(BF16), w2_ref[0], preferred_element_type=F32)


def _compress(proj3, pos, w1r, w2b):
    B, T, _ = proj3.shape
    ng = T // CMP_STRIDE
    return pl.pallas_call(
        _compress_kernel,
        grid=(B, NSA_G, 2),
        in_specs=[
            pl.BlockSpec((1, T, LANE), lambda b, g, j: (b, 0, CB_KCM + 2 * j + g)),
            pl.BlockSpec((1, CMP_BLOCK, NSA_D), lambda b, g, j: (j, 0, 0)),
            pl.BlockSpec((1, CMP_BLOCK, NSA_D, NSA_D), lambda b, g, j: (j, 0, 0, 0)),
            pl.BlockSpec((1, NSA_D, NSA_D), lambda b, g, j: (j, 0, 0)),
        ],
        out_specs=pl.BlockSpec((1, 1, 1, ng, NSA_D), lambda b, g, j: (b, g, j, 0, 0)),
        out_shape=jax.ShapeDtypeStruct((B, NSA_G, 2, ng, NSA_D), F32),
        compiler_params=_cparams(("parallel", "parallel", "parallel")),
        name="nsa_compress",
    )(proj3, pos, w1r, w2b)


def _overlap_matrix(T):
    n_cmp = (T - CMP_BLOCK) // CMP_STRIDE + 1
    n_sel = T // SEL_BLOCK
    ratio_s = SEL_BLOCK // CMP_STRIDE
    ratio_c = CMP_BLOCK // CMP_STRIDE
    jj, mm, nn = np.meshgrid(np.arange(n_sel), np.arange(ratio_s), np.arange(ratio_c), indexing='ij')
    ii = ratio_s * jj + mm + nn - (ratio_c - 1)
    ok = (ii >= 0) & (ii < n_cmp)
    overlap = np.zeros((T // CMP_STRIDE, LANE), np.float32)
    np.add.at(overlap, (ii[ok], jj[ok]), 1.0)
    return overlap


def _expand_matrix(T):
    e = np.zeros((LANE, T), np.float32)
    e[np.arange(T) // SEL_BLOCK, np.arange(T)] = 1.0
    return e


def _nsa_kernel(slopes_ref, q_ref, ks_ref, vs_ref, kw_ref, vw_ref, gt_ref, kvc_ref, ov_ref, e_ref, ng_ref,
                o_ref, ksb, vsb, kwb, vwb, selk_ref):
    T = ks_ref.shape[1]
    R, D = NSA_R, NSA_D
    N = R * TQ
    n_sel = T // SEL_BLOCK
    top_n = min(SEL_TOP, n_sel)
    g = pl.program_id(1)
    qi = pl.program_id(2)
    q0 = qi * TQ

    @pl.when(qi == 0)
    def _():
        cr = 256

        def body(r, c):
            sl = pl.ds(pl.multiple_of(r * cr, cr), cr)
            ksb[sl, :] = ks_ref[0, sl, :].astype(BF16)
            vsb[sl, :] = vs_ref[0, sl, :].astype(BF16)
            kwb[sl, :] = kw_ref[0, sl, :].astype(BF16)
            vwb[sl, :] = vw_ref[0, sl, :].astype(BF16)
            return c
        lax.fori_loop(0, T // cr, body, 0)

    def stack(x):
        return jnp.concatenate([x] * R, axis=0)

    q = q_ref[0] * (D ** -0.5)
    qs = jnp.concatenate([q[:, r * D:(r + 1) * D] for r in range(R)], axis=0).astype(BF16)
    slope = jnp.concatenate([jnp.full((TQ, 1), slopes_ref[g, r], F32) for r in range(R)], axis=0)
    t1 = q0 + lax.broadcasted_iota(jnp.int32, (TQ, 1), 0)
    trow = stack(t1)

    kc = kvc_ref[0, 0, 0].astype(BF16)
    vc = kvc_ref[0, 0, 1].astype(BF16)
    ncmp = kc.shape[0]
    cend = lax.broadcasted_iota(jnp.int32, (1, ncmp), 1) * CMP_STRIDE + (CMP_BLOCK - 1)
    cend = jnp.where(cend < T, cend, 4 * T)
    valid_c = trow >= cend
    s = lax.dot_general(qs, kc, (((1,), (1,)), ((), ())), preferred_element_type=F32)
    s = s + slope * cend.astype(F32)
    s = jnp.where(valid_c, s, NEG)
    mx = jnp.max(s, axis=-1, keepdims=True)
    e = jnp.where(valid_c, jnp.exp(s - mx), 0.0)
    l = jnp.sum(e, axis=-1, keepdims=True)
    p_cmp = e / jnp.where(l > 0.0, l, 1.0)
    o_cmp = jnp.dot(p_cmp.astype(BF16), vc, preferred_element_type=F32)

    psum = p_cmp[0:TQ]
    for r in range(1, R):
        psum = psum + p_cmp[r * TQ:(r + 1) * TQ]
    ph = psum.astype(BF16)
    plo = (psum - ph.astype(F32)).astype(BF16)
    ov = ov_ref[...]
    imp = jnp.dot(ph, ov, preferred_element_type=F32) + jnp.dot(plo, ov, preferred_element_type=F32)

    blk = lax.broadcasted_iota(jnp.int32, (TQ, LANE), 1)
    cur = t1 // SEL_BLOCK
    forced = (blk == 0) | (blk == cur) | (blk == cur - 1)
    score = jnp.where(forced, FORCED_SCORE, jnp.where(blk <= cur, imp, -FORCED_SCORE))
    score = jnp.where(blk < n_sel, score, -4.0 * FORCED_SCORE)
    rank = jnp.zeros((TQ, LANE), F32)
    for kk in range(n_sel):
        sk = score[:, kk:kk + 1]
        beats = jnp.where(sk > score, 1.0, jnp.where(sk == score, jnp.where(blk > kk, 1.0, 0.0), 0.0))
        rank = rank + beats
    selmask = jnp.where(rank < float(top_n), 1.0, 0.0).astype(BF16)
    for c in range(T // KC):
        selk_ref[c] = jnp.dot(selmask, e_ref[:, c * KC:(c + 1) * KC], preferred_element_type=F32)

    rloc = lax.broadcasted_iota(jnp.int32, (N, KC), 0) % TQ
    rel = rloc - lax.broadcasted_iota(jnp.int32, (N, KC), 1)
    srel = slope * rel.astype(F32)

    def sel_body(c, carry):
        m, lsum, acc = carry
        k0 = pl.multiple_of(c * KC, KC)
        kch = ksb[pl.ds(k0, KC), :]
        vch = vsb[pl.ds(k0, KC), :]
        s1 = lax.dot_general(qs, kch, (((1,), (1,)), ((), ())), preferred_element_type=F32) - srel
        selm = stack(selk_ref[c])
        valid = jnp.where(rel >= k0 - q0, selm, 0.0) > 0.5
        cc = slope * (k0 - q0).astype(F32)
        sm = jnp.where(valid, s1, NEG)
        m_new = jnp.maximum(m, jnp.max(sm, axis=-1, keepdims=True) + cc)
        p = jnp.where(valid, jnp.exp(sm - (m_new - cc)), 0.0)
        alpha = jnp.exp(m - m_new)
        lsum = alpha * lsum + jnp.sum(p, axis=-1, keepdims=True)
        acc = alpha * acc + jnp.dot(p.astype(BF16), vch, preferred_element_type=F32)
        return m_new, lsum, acc

    n_ch = (q0 + TQ - 1) // KC + 1
    m0 = jnp.full((N, 1), NEG, F32)
    _, l_sel, acc_sel = lax.fori_loop(0, n_ch, sel_body, (m0, jnp.zeros((N, 1), F32), jnp.zeros((N, D), F32)))
    o_sel = acc_sel / l_sel

    start = pl.multiple_of(jnp.maximum(q0 - WINDOW, 0), TQ)
    kw = kwb[pl.ds(start, WSPAN), :]
    vw = vwb[pl.ds(start, WSPAN), :]
    relw = (lax.broadcasted_iota(jnp.int32, (N, WSPAN), 0) % TQ) - lax.broadcasted_iota(jnp.int32, (N, WSPAN), 1)
    dist = relw + (q0 - start)
    sw = lax.dot_general(qs, kw, (((1,), (1,)), ((), ())), preferred_element_type=F32) - slope * relw.astype(F32)
    valid_w = jnp.where(dist >= 0, jnp.where(dist < WINDOW, 1.0, 0.0), 0.0) > 0.5
    sw = jnp.where(valid_w, sw, NEG)
    mw = jnp.max(sw, axis=-1, keepdims=True)
    pw = jnp.where(valid_w, jnp.exp(sw - mw), 0.0)
    lw = jnp.sum(pw, axis=-1, keepdims=True)
    o_win = jnp.dot(pw.astype(BF16), vw, preferred_element_type=F32) / lw

    sig = jax.nn.sigmoid(gt_ref[0])
    lane = lax.broadcasted_iota(jnp.int32, (TQ, LANE), 1)

    def gate_col(br):
        cols = []
        for r in range(R):
            cidx = (g * R + r) * 3 + br
            cols.append(jnp.sum(jnp.where(lane == cidx, sig, 0.0), axis=-1, keepdims=True))
        return jnp.concatenate(cols, axis=0)

    o = gate_col(0) * o_cmp + gate_col(1) * o_sel + gate_col(2) * o_win
    y = _rms_rows(o, ng_ref[...])
    for r in range(R):
        o_ref[0, :, r * D:(r + 1) * D] = y[r * TQ:(r + 1) * TQ].astype(o_ref.dtype)


def _nsa(proj3, kvc, slopes, ov, ex, ng):
    B, T, _ = proj3.shape
    ncmp = T // CMP_STRIDE

    def kv(cb):
        return pl.BlockSpec((1, T, LANE), lambda b, g, qi, s, cb=cb: (b, 0, cb + g))

    grid_spec = pltpu.PrefetchScalarGridSpec(
        num_scalar_prefetch=1,
        grid=(B, NSA_G, T // TQ),
        in_specs=[
            pl.BlockSpec((1, TQ, NSA_R * NSA_D), lambda b, g, qi, s: (b, qi, CB_NQ // NSA_R + g)),
            kv(CB_KSL), kv(CB_VSL), kv(CB_KWN), kv(CB_VWN),
            pl.BlockSpec((1, TQ, LANE), lambda b, g, qi, s: (b, qi, CB_GATE)),
            pl.BlockSpec((1, 1, 2, ncmp, NSA_D), lambda b, g, qi, s: (b, g, 0, 0, 0)),
            pl.BlockSpec((ncmp, LANE), lambda b, g, qi, s: (0, 0)),
            pl.BlockSpec((LANE, T), lambda b, g, qi, s: (0, 0)),
            pl.BlockSpec((1, NSA_D), lambda b, g, qi, s: (0, 0)),
        ],
        out_specs=pl.BlockSpec((1, TQ, NSA_R * NSA_D), lambda b, g, qi, s: (b, qi, g)),
        scratch_shapes=[pltpu.VMEM((T, NSA_D), BF16)] * 4 + [pltpu.VMEM((T // KC, TQ, KC), F32)],
    )
    return pl.pallas_call(
        _nsa_kernel,
        grid_spec=grid_spec,
        out_shape=jax.ShapeDtypeStruct((B, T, NSA_WIDTH), BF16),
        compiler_params=_cparams(("parallel", "parallel", "arbitrary")),
        name="nsa_attention",
    )(slopes, proj3, proj3, proj3, proj3, proj3, proj3, kvc, ov, ex, ng)


def _outproj_kernel(hg_ref, ns_ref, x_ref, wo_ref, g2_ref, wr_ref, br_ref,
                    h1_ref, xs_ref, idx_ref, w_ref, xn_ref):
    h1_ref[...] = (x_ref[...]
                   + jnp.dot(hg_ref[...], wo_ref[0:HG_WIDTH, :], preferred_element_type=F32)
                   + jnp.dot(ns_ref[...], wo_ref[HG_WIDTH:, :], preferred_element_type=F32))

    def body(r, c):
        r0 = pl.multiple_of(r * NORM_ROWS, NORM_ROWS)
        xn = _rms_rows(h1_ref[pl.ds(r0, NORM_ROWS), :], g2_ref[...])
        xn_ref[pl.ds(r0, NORM_ROWS), :] = xn
        for s in range(SEG):
            xs_ref[pl.ds(r0 * SEG + s, NORM_ROWS, stride=SEG), :] = xn[:, s * LANE:(s + 1) * LANE]
        return c
    lax.fori_loop(0, TM_OUT // NORM_ROWS, body, 0)

    logits = jnp.dot(xn_ref[...], wr_ref[...], preferred_element_type=F32) + br_ref[...]
    lane = lax.broadcasted_iota(jnp.int32, logits.shape, 1)
    lg = jnp.where(lane < N_EXPERTS, logits, -jnp.inf)
    vals, idxs = [], []
    for _ in range(TOP_K):
        mx = jnp.max(lg, axis=-1, keepdims=True)
        ix = jnp.min(jnp.where(lg == mx, lane, LANE), axis=-1, keepdims=True)
        vals.append(mx)
        idxs.append(ix)
        lg = jnp.where(lane == ix, -jnp.inf, lg)
    es = [jnp.exp(v - vals[0]) for v in vals]
    den = es[0]
    for e in es[1:]:
        den = den + e
    idx_out = jnp.zeros(logits.shape, jnp.int32)
    w_out = jnp.zeros(logits.shape, F32)
    for k in range(TOP_K):
        idx_out = jnp.where(lane == k, idxs[k], idx_out)
        w_out = jnp.where(lane == k, es[k] / den, w_out)
    idx_ref[...] = idx_out
    w_ref[...] = w_out


def _outproj(y_hg, y_nsa, x2, wo, g2, wr_pad, br_pad):
    n_tok = x2.shape[0]
    row = lambda w: pl.BlockSpec((TM_OUT, w), lambda i: (i, 0))
    full = lambda a, b: pl.BlockSpec((a, b), lambda i: (0, 0))
    return pl.pallas_call(
        _outproj_kernel,
        grid=(n_tok // TM_OUT,),
        in_specs=[row(HG_WIDTH), row(NSA_WIDTH), row(D_MODEL), full(D_MODEL, D_MODEL), full(1, D_MODEL),
                  full(D_MODEL, LANE), full(1, LANE)],
        out_specs=[row(D_MODEL), pl.BlockSpec((TM_OUT * SEG, LANE), lambda i: (i, 0)), row(LANE), row(LANE)],
        out_shape=[jax.ShapeDtypeStruct((n_tok, D_MODEL), F32), jax.ShapeDtypeStruct((n_tok * SEG, LANE), F32),
                   jax.ShapeDtypeStruct((n_tok, LANE), jnp.int32), jax.ShapeDtypeStruct((n_tok, LANE), F32)],
        scratch_shapes=[pltpu.VMEM((TM_OUT, D_MODEL), F32)],
        compiler_params=_cparams(("parallel",)),
        name="outproj_router",
    )(y_hg, y_nsa, x2, wo, g2, wr_pad, br_pad)


def _moe_kernel(e_ref, sbase_ref, rows_ref, nu_ref, stok_ref,
                xs_hbm, wg_ref, wu_ref, wd_ref, bg_ref, bu_ref, bd_ref,
                y_ref, xstage, xg, act, wgub, wdb, sem):
    i = pl.program_id(0)
    s = pl.program_id(1)
    nu = nu_ref[0]
    n_asg = stok_ref.shape[0]
    nsub = (rows_ref[i] + TS_MOE - 1) // TS_MOE
    unroll = 8

    def row_copy(tok, r):
        return pltpu.make_async_copy(xs_hbm.at[pl.ds(tok * SEG, SEG)], xstage.at[pl.ds(r * SEG, SEG)], sem.at[0])

    def issue(blk):
        base = sbase_ref[blk]
        nrow = (rows_ref[blk] + TS_MOE - 1) // TS_MOE * TS_MOE

        def body(r8, c):
            for u in range(unroll):
                r = r8 * unroll + u
                row_copy(stok_ref[jnp.minimum(base + r, n_asg - 1)], r).start()
            return c
        lax.fori_loop(0, nrow // unroll, body, 0)

    @pl.when(s == 0)
    def _():
        @pl.when(i == 0)
        def _():
            issue(0)

        @pl.when(i < nu)
        def _():
            nrow = nsub * TS_MOE
            pltpu.make_async_copy(xs_hbm.at[pl.ds(0, nrow * SEG)], xstage.at[pl.ds(0, nrow * SEG)], sem.at[0]).wait()

            def cbody(r, c):
                r0 = pl.multiple_of(r * 64, 64)
                for sg in range(SEG):
                    xg[pl.ds(r0, 64), sg * LANE:(sg + 1) * LANE] = (
                        xstage[pl.ds(r0 * SEG + sg, 64, stride=SEG), :].astype(BF16))
                return c
            lax.fori_loop(0, nrow // 64, cbody, 0)

        @pl.when(i + 1 < nu)
        def _():
            issue(i + 1)

    def sub_rows(j):
        return pl.ds(pl.multiple_of(j * TS_MOE, TS_MOE), TS_MOE)

    @pl.when((s < NF_MOE) & (i < nu))
    def _():
        def wb(r, c):
            sl = pl.ds(pl.multiple_of(r * 256, 256), 256)
            wgub[sl, 0:FC_MOE] = wg_ref[0, sl, :].astype(BF16)
            wgub[sl, FC_MOE:2 * FC_MOE] = wu_ref[0, sl, :].astype(BF16)
            return c
        lax.fori_loop(0, D_MODEL // 256, wb, 0)
        bg = bg_ref[0]
        bu = bu_ref[0]

        def sub(j, c):
            rs = sub_rows(j)
            hgu = jnp.dot(xg[rs, :], wgub[...], preferred_element_type=F32)
            gate = jnp.minimum(hgu[:, 0:FC_MOE] + bg, SWIGLU_LIMIT)
            up = jnp.clip(hgu[:, FC_MOE:2 * FC_MOE] + bu, -SWIGLU_LIMIT, SWIGLU_LIMIT)
            a = gate * jax.nn.sigmoid(SWIGLU_ALPHA * gate) * (up + 1.0)
            act[s, rs, :] = a.astype(BF16)
            return c
        lax.fori_loop(0, nsub, sub, 0)

    @pl.when((s >= NF_MOE) & (i < nu))
    def _():
        def wb(r, c):
            sl = pl.ds(pl.multiple_of(r * 256, 256), 256)
            wdb[sl, :] = wd_ref[0, sl, :].astype(BF16)
            return c
        lax.fori_loop(0, D_FF // 256, wb, 0)
        bd = bd_ref[0]

        def sub(j, c):
            rs = sub_rows(j)
            acc = jnp.zeros((TS_MOE, FC_MOE), F32) + bd
            for f in range(NF_MOE):
                acc = acc + jnp.dot(act[f, rs, :], wdb[f * FC_MOE:(f + 1) * FC_MOE, :], preferred_element_type=F32)
            y_ref[rs, :] = acc
            return c
        lax.fori_loop(0, nsub, sub, 0)

        def zfill(j, c):
            y_ref[sub_rows(j), :] = jnp.zeros((TS_MOE, FC_MOE), F32)
            return c
        lax.fori_loop(nsub, SB_MOE // TS_MOE, zfill, 0)

    @pl.when((s >= NF_MOE) & (i >= nu))
    def _():
        y_ref[...] = jnp.zeros_like(y_ref)


def _moe(sup_e, sup_base, sup_rows, n_used, stok, xs, wgu, wd, bgu, bd):
    ns = sup_e.shape[0]

    def e_of(i, e, nu):
        return e[jnp.where(i < nu[0], i, nu[0] - 1)]

    def f_of(i, s, nu):
        return jnp.where(i < nu[0], jnp.minimum(s, NF_MOE - 1), NF_MOE - 1)

    def n_of(i, s, nu):
        return jnp.where(i < nu[0], jnp.maximum(s - NF_MOE, 0), NN_MOE - 1)

    def wg_map(i, s, e, sb, rw, nu, tk):
        return (e_of(i, e, nu), 0, f_of(i, s, nu))

    def wu_map(i, s, e, sb, rw, nu, tk):
        return (e_of(i, e, nu), 0, NF_MOE + f_of(i, s, nu))

    def wd_map(i, s, e, sb, rw, nu, tk):
        return (e_of(i, e, nu), 0, n_of(i, s, nu))

    def y_map(i, s, e, sb, rw, nu, tk):
        return (i, jnp.maximum(s - NF_MOE, 0))

    grid_spec = pltpu.PrefetchScalarGridSpec(
        num_scalar_prefetch=5,
        grid=(ns, NF_MOE + NN_MOE),
        in_specs=[
            pl.BlockSpec(memory_space=pl.ANY),
            pl.BlockSpec((1, D_MODEL, FC_MOE), wg_map),
            pl.BlockSpec((1, D_MODEL, FC_MOE), wu_map),
            pl.BlockSpec((1, D_FF, FC_MOE), wd_map),
            pl.BlockSpec((1, 1, FC_MOE), wg_map),
            pl.BlockSpec((1, 1, FC_MOE), wu_map),
            pl.BlockSpec((1, 1, FC_MOE), wd_map),
        ],
        out_specs=pl.BlockSpec((SB_MOE, FC_MOE), y_map),
        scratch_shapes=[pltpu.VMEM((SB_MOE * SEG, LANE), F32), pltpu.VMEM((SB_MOE, D_MODEL), BF16),
                        pltpu.VMEM((NF_MOE, SB_MOE, FC_MOE), BF16), pltpu.VMEM((D_MODEL, 2 * FC_MOE), BF16),
                        pltpu.VMEM((D_FF, FC_MOE), BF16), pltpu.SemaphoreType.DMA((1,))],
    )
    return pl.pallas_call(
        _moe_kernel,
        grid_spec=grid_spec,
        out_shape=jax.ShapeDtypeStruct((ns * SB_MOE, D_MODEL), F32),
        compiler_params=_cparams(("arbitrary", "arbitrary")),
        name="moe_experts",
    )(sup_e, sup_base, sup_rows, n_used, stok, xs, wgu, wgu, wd, bgu, bgu, bd)


def _combine_kernel(pos_ref, y_hbm, h1_ref, w_ref, g_ref, o_ref, ybuf, sem):
    i = pl.program_id(0)
    n = pl.num_programs(0)
    slot = i % 2

    unroll = 4

    def row_copy(src, sl, k, r):
        return pltpu.make_async_copy(y_hbm.at[src], ybuf.at[sl, k * TC_CMB + r], sem.at[sl])

    def issue(blk, sl):
        def body(r4, c):
            for u in range(unroll):
                r = r4 * unroll + u
                for k in range(TOP_K):
                    row_copy(pos_ref[(blk * TC_CMB + r) * TOP_K + k], sl, k, r).start()
            return c
        lax.fori_loop(0, TC_CMB // unroll, body, 0)

    @pl.when(i == 0)
    def _():
        issue(0, 0)

    pltpu.make_async_copy(y_hbm.at[pl.ds(0, TOP_K * TC_CMB)], ybuf.at[slot], sem.at[slot]).wait()

    @pl.when(i + 1 < n)
    def _():
        issue(i + 1, 1 - slot)

    w = w_ref[...]
    h = h1_ref[...]
    for k in range(TOP_K):
        h = h + w[:, k:k + 1] * ybuf[slot, k * TC_CMB:(k + 1) * TC_CMB, :]
    o_ref[...] = _rms_rows(h, g_ref[...])


def _combine(pos, y_sorted, h1, top_w, g):
    n_tok = h1.shape[0]
    grid_spec = pltpu.PrefetchScalarGridSpec(
        num_scalar_prefetch=1,
        grid=(n_tok // TC_CMB,),
        in_specs=[
            pl.BlockSpec(memory_space=pl.ANY),
            pl.BlockSpec((TC_CMB, D_MODEL), lambda i, p: (i, 0)),
            pl.BlockSpec((TC_CMB, LANE), lambda i, p: (i, 0)),
            pl.BlockSpec((1, D_MODEL), lambda i, p: (0, 0)),
        ],
        out_specs=pl.BlockSpec((TC_CMB, D_MODEL), lambda i, p: (i, 0)),
        scratch_shapes=[pltpu.VMEM((2, TOP_K * TC_CMB, D_MODEL), F32), pltpu.SemaphoreType.DMA((2,))],
    )
    return pl.pallas_call(
        _combine_kernel,
        grid_spec=grid_spec,
        out_shape=jax.ShapeDtypeStruct((n_tok, D_MODEL), F32),
        compiler_params=_cparams(("arbitrary",)),
        name="moe_combine",
    )(pos, y_sorted, h1, top_w, g)


def _dispatch(top_idx):
    n_tok = top_idx.shape[0]
    n_asg = n_tok * TOP_K
    i32 = jnp.int32
    flat_e = top_idx.reshape(n_asg)
    order = jnp.argsort(flat_e).astype(i32)
    inv = jnp.argsort(order).astype(i32)
    stok = order // TOP_K
    eids = jnp.arange(N_EXPERTS, dtype=i32)
    counts = jnp.sum((flat_e[:, None] == eids[None, :]).astype(i32), axis=0)
    start = jnp.cumsum(counts) - counts
    nsup = (counts + SB_MOE - 1) // SB_MOE
    sup_end = jnp.cumsum(nsup)
    sup_start = sup_end - nsup
    per = jnp.where(nsup > 0, (counts + nsup * TS_MOE - 1) // jnp.maximum(nsup * TS_MOE, 1) * TS_MOE, 0)
    ns = -(-n_asg // SB_MOE) + N_EXPERTS
    ii = jnp.arange(ns, dtype=i32)
    sup_e = jnp.minimum(jnp.sum((sup_end[None, :] <= ii[:, None]).astype(i32), axis=1), N_EXPERTS - 1)
    k_i = ii - sup_start[sup_e]
    sup_base = (start[sup_e] + k_i * per[sup_e]).astype(i32)
    sup_rows = jnp.clip(counts[sup_e] - k_i * per[sup_e], 0, per[sup_e]).astype(i32)
    n_used = sup_end[-1:].astype(i32)
    rank = inv - start[flat_e]
    pe = jnp.maximum(per[flat_e], 1)
    pos = ((sup_start[flat_e] + rank // pe) * SB_MOE + rank % pe).astype(i32)
    return sup_e.astype(i32), sup_base, sup_rows, n_used, stok, pos


def _alibi_slopes():
    h = np.arange(1, NSA_HEADS + 1, dtype=np.float32)
    return np.power(np.float32(2.0), -8.0 * h / NSA_HEADS).astype(np.float32).reshape(NSA_G, NSA_R)


def kernel(x, norm1_g, w_in, hg_lb_logits, hg_norm_g, cmp_pos, cmp_w1, cmp_w2, nsa_norm_g, w_out,
           norm2_g, w_router, b_router, w_gate_up, b_gate_up, w_down, b_down, final_norm_g):
    B, T, D = x.shape
    n_tok = B * T
    l = 0
    x2 = x.reshape(n_tok, D)

    w_in_pad = jnp.pad(w_in[l], ((0, 0), (0, IN_COLS_PAD - IN_COLS))).astype(BF16)
    lb_all = jnp.cumsum(jax.nn.softmax(hg_lb_logits.astype(F32), axis=0), axis=0)
    lb = lb_all[l].reshape(1, HG_WIDTH)
    w1r = cmp_w1[l].reshape(2, CMP_BLOCK, NSA_D, NSA_D).astype(BF16)
    w2b = cmp_w2[l].astype(BF16)
    wo = w_out[l].astype(BF16)
    wr_pad = jnp.pad(w_router[l], ((0, 0), (0, LANE - N_EXPERTS)))
    br_pad = jnp.pad(b_router[l], (0, LANE - N_EXPERTS)).reshape(1, LANE)
    bgu = b_gate_up[l].reshape(N_EXPERTS, 1, 2 * D_FF)
    bd = b_down[l].reshape(N_EXPERTS, 1, D_MODEL)

    proj = _inproj(x2, norm1_g[l].reshape(1, D), w_in_pad)
    proj3 = proj.reshape(B, T, IN_COLS_PAD)
    y_hg = _hgrn(proj3, lb, hg_norm_g[l].reshape(1, HG_DK))
    kvc = _compress(proj3, cmp_pos[l], w1r, w2b)
    y_nsa = _nsa(proj3, kvc, jnp.asarray(_alibi_slopes()), jnp.asarray(_overlap_matrix(T), BF16),
                 jnp.asarray(_expand_matrix(T), BF16), nsa_norm_g[l].reshape(1, NSA_D))
    h1, xs, top_idx, top_w = _outproj(y_hg.reshape(n_tok, HG_WIDTH), y_nsa.reshape(n_tok, NSA_WIDTH), x2, wo,
                                      norm2_g[l].reshape(1, D), wr_pad, br_pad)
    sup_e, sup_base, sup_rows, n_used, stok, pos = _dispatch(top_idx[:, :TOP_K])
    y_sorted = _moe(sup_e, sup_base, sup_rows, n_used, stok, xs, w_gate_up[l], w_down[l], bgu, bd)
    out = _combine(pos, y_sorted, h1, top_w, final_norm_g.reshape(1, D))
    return out.reshape(B, T, D)
```

```python
import functools

import numpy as np
import jax
import jax.numpy as jnp
from jax import lax
from jax.experimental import pallas as pl
from jax.experimental.pallas import tpu as pltpu

F32 = jnp.float32
BF16 = jnp.bfloat16

D_MODEL = 2048
HG_WIDTH = 1024
HG_DK = 128
HG_HEADS = 8
HG_CHUNK = 64
NSA_WIDTH = 1024
NSA_D = 128
NSA_HEADS = 8
NSA_G = 2
NSA_R = 4
KV_WIDTH = NSA_G * NSA_D
CMP_BLOCK = 32
CMP_STRIDE = 16
SEL_BLOCK = 64
SEL_TOP = 16
WINDOW = 512
N_EXPERTS = 32
TOP_K = 4
D_FF = 2048
SWIGLU_LIMIT = 7.0
SWIGLU_ALPHA = 1.702
EPS = 1e-6
NEG = -1e30
FORCED_SCORE = 1e4
IN_COLS = 4 * HG_WIDTH + NSA_WIDTH + 6 * KV_WIDTH + 3 * NSA_HEADS

LANE = 128
CB_HQ, CB_HF, CB_HI, CB_HG = 0, 8, 16, 24
CB_NQ = 32
CB_KCM, CB_VCM, CB_KSL, CB_VSL, CB_KWN, CB_VWN = 40, 42, 44, 46, 48, 50
CB_GATE = 52

TM_IN = 1024
TN_IN = 768
IN_COLS_PAD = 9 * TN_IN
NORM_ROWS = 32
HG_TT = 256
TQ = 128
KC = 512
WSPAN = WINDOW + TQ
TM_OUT = 512
SEG = D_MODEL // LANE
SB_MOE = 1280
TS_MOE = 512
RQ_MOE = 128
FC_MOE = 256
FCB_MOE = 512
NF_MOE = D_FF // FC_MOE
NN_MOE = D_MODEL // FCB_MOE
TC_CMB = 128
VMEM_LIMIT = 56 * 1024 * 1024


def _cparams(sem):
    return pltpu.CompilerParams(dimension_semantics=sem, vmem_limit_bytes=VMEM_LIMIT)


def _rms_rows(x, g):
    ms = jnp.mean(x * x, axis=-1, keepdims=True)
    return x * lax.rsqrt(ms + EPS) * g


def _inproj_kernel(x_ref, g_ref, w_ref, o_ref, u_ref):
    @pl.when(pl.program_id(1) == 0)
    def _():
        def body(r, c):
            sl = pl.ds(pl.multiple_of(r * NORM_ROWS, NORM_ROWS), NORM_ROWS)
            u_ref[sl, :] = _rms_rows(x_ref[sl, :], g_ref[...]).astype(BF16)
            return c
        lax.fori_loop(0, TM_IN // NORM_ROWS, body, 0)

    o_ref[...] = jnp.dot(u_ref[...], w_ref[...], preferred_element_type=F32)


def _inproj(x2, g, w_pad):
    n_tok = x2.shape[0]
    return pl.pallas_call(
        _inproj_kernel,
        grid=(n_tok // TM_IN, IN_COLS_PAD // TN_IN),
        in_specs=[
            pl.BlockSpec((TM_IN, D_MODEL), lambda i, j: (i, 0)),
            pl.BlockSpec((1, D_MODEL), lambda i, j: (0, 0)),
            pl.BlockSpec((D_MODEL, TN_IN), lambda i, j: (0, j)),
        ],
        out_specs=pl.BlockSpec((TM_IN, TN_IN), lambda i, j: (i, j)),
        out_shape=jax.ShapeDtypeStruct((n_tok, IN_COLS_PAD), F32),
        scratch_shapes=[pltpu.VMEM((TM_IN, D_MODEL), BF16)],
        compiler_params=_cparams(("parallel", "arbitrary")),
        name="inproj",
    )(x2, g, w_pad)


def _hgrn_kernel(q_ref, f_ref, i_ref, g_ref, lb_ref, ng_ref, o_ref, st_ref):
    C = HG_CHUNK
    ng = ng_ref[...]
    row = lax.broadcasted_iota(jnp.int32, (C, C), 0)
    col = lax.broadcasted_iota(jnp.int32, (C, C), 1)
    tril = row >= col
    tri_bf = jnp.where(tril, 1.0, 0.0).astype(BF16)

    @pl.when(pl.program_id(1) == 0)
    def _():
        st_ref[...] = jnp.zeros_like(st_ref)

    def body(c, carry):
        sl = pl.ds(pl.multiple_of(c * C, C), C)
        for h in range(HG_HEADS):
            hs = slice(h * HG_DK, (h + 1) * HG_DK)
            lb = lb_ref[:, hs]
            st = st_ref[h]
            f = lb + (1.0 - lb) * jax.nn.sigmoid(f_ref[0, sl, hs])
            logf = jnp.log(f)
            k = 1.0 - f
            qf = jax.nn.silu(q_ref[0, sl, hs])
            v_bf = i_ref[0, sl, hs].astype(BF16)
            hi = logf.astype(BF16)
            lo = (logf - hi.astype(F32)).astype(BF16)
            b = jnp.dot(tri_bf, hi, preferred_element_type=F32) + jnp.dot(tri_bf, lo, preferred_element_type=F32)
            b_last = b[C - 1:C, :]
            q_dec = (qf * jnp.exp(b)).astype(BF16)
            k_dec = (k * jnp.exp(-b)).astype(BF16)
            a = lax.dot_general(q_dec, k_dec, (((1,), (1,)), ((), ())), preferred_element_type=F32)
            a = jnp.where(tril, a, 0.0).astype(BF16)
            o = jnp.dot(a, v_bf, preferred_element_type=F32)
            o = o + lax.dot_general(q_dec, st.astype(BF16), (((1,), (1,)), ((), ())), preferred_element_type=F32)
            k_rem = (k * jnp.exp(b_last - b)).astype(BF16)
            st_ref[h] = st * jnp.exp(b_last) + lax.dot_general(v_bf, k_rem, (((0,), (0,)), ((), ())),
                                                                 preferred_element_type=F32)
            y = _rms_rows(o, ng) * jax.nn.silu(g_ref[0, sl, hs])
            o_ref[0, sl, hs] = y.astype(o_ref.dtype)
        return carry

    lax.fori_loop(0, HG_TT // C, body, 0)


def _hgrn(proj3, lb, ng):
    B, T, _ = proj3.shape

    def col(cb):
        return pl.BlockSpec((1, HG_TT, HG_WIDTH), lambda b, t, cb=cb: (b, t, cb // HG_HEADS))

    return pl.pallas_call(
        _hgrn_kernel,
        grid=(B, T // HG_TT),
        in_specs=[col(CB_HQ), col(CB_HF), col(CB_HI), col(CB_HG),
                  pl.BlockSpec((1, HG_WIDTH), lambda b, t: (0, 0)),
                  pl.BlockSpec((1, LANE), lambda b, t: (0, 0))],
        out_specs=pl.BlockSpec((1, HG_TT, HG_WIDTH), lambda b, t: (b, t, 0)),
        out_shape=jax.ShapeDtypeStruct((B, T, HG_WIDTH), BF16),
        scratch_shapes=[pltpu.VMEM((HG_HEADS, HG_DK, HG_DK), F32)],
        compiler_params=_cparams(("parallel", "arbitrary")),
        name="hgrn2",
    )(proj3, proj3, proj3, proj3, lb, ng)


def _gelu_tanh(x):
    return 0.5 * x * (1.0 + jnp.tanh(0.7978845608028654 * (x + 0.044715 * (x * x * x))))


def _compress_kernel(x_ref, pos_ref, w1_ref, w2_ref, o_ref):
    T = x_ref.shape[1]
    ng = T // CMP_STRIDE
    acc_a = jnp.zeros((ng, NSA_D), F32)
    acc_b = jnp.zeros((ng, NSA_D), F32)
    for r in range(CMP_STRIDE):
        xr = x_ref[0, pl.ds(r, ng, stride=CMP_STRIDE), :]
        xa = (xr + pos_ref[0, r:r + 1, :]).astype(BF16)
        xb = (xr + pos_ref[0, CMP_STRIDE + r:CMP_STRIDE + r + 1, :]).astype(BF16)
        acc_a = acc_a + jnp.dot(xa, w1_ref[0, r], preferred_element_type=F32)
        acc_b = acc_b + jnp.dot(xb, w1_ref[0, CMP_STRIDE + r], preferred_element_type=F32)
    hdn = _gelu_tanh(acc_a + pltpu.roll(acc_b, ng - 1, 0))
    o_ref[0, 0, 0] = jnp.dot(hdn.astype(BF16), w2_ref[0], preferred_element_type=F32)


def _compress(proj3, pos, w1r, w2b):
    B, T, _ = proj3.shape
    ng = T // CMP_STRIDE
    return pl.pallas_call(
        _compress_kernel,
        grid=(B, NSA_G, 2),
        in_specs=[
            pl.BlockSpec((1, T, LANE), lambda b, g, j: (b, 0, CB_KCM + 2 * j + g)),
            pl.BlockSpec((1, CMP_BLOCK, NSA_D), lambda b, g, j: (j, 0, 0)),
            pl.BlockSpec((1, CMP_BLOCK, NSA_D, NSA_D), lambda b, g, j: (j, 0, 0, 0)),
            pl.BlockSpec((1, NSA_D, NSA_D), lambda b, g, j: (j, 0, 0)),
        ],
        out_specs=pl.BlockSpec((1, 1, 1, ng, NSA_D), lambda b, g, j: (b, g, j, 0, 0)),
        out_shape=jax.ShapeDtypeStruct((B, NSA_G, 2, ng, NSA_D), F32),
        compiler_params=_cparams(("parallel", "parallel", "parallel")),
        name="nsa_compress",
    )(proj3, pos, w1r, w2b)


def _overlap_matrix(T):
    n_cmp = (T - CMP_BLOCK) // CMP_STRIDE + 1
    n_sel = T // SEL_BLOCK
    ratio_s = SEL_BLOCK // CMP_STRIDE
    ratio_c = CMP_BLOCK // CMP_STRIDE
    jj, mm, nn = np.meshgrid(np.arange(n_sel), np.arange(ratio_s), np.arange(ratio_c), indexing='ij')
    ii = ratio_s * jj + mm + nn - (ratio_c - 1)
    ok = (ii >= 0) & (ii < n_cmp)
    overlap = np.zeros((T // CMP_STRIDE, LANE), np.float32)
    np.add.at(overlap, (ii[ok], jj[ok]), 1.0)
    return overlap


def _expand_matrix(T):
    e = np.zeros((LANE, T), np.float32)
    e[np.arange(T) // SEL_BLOCK, np.arange(T)] = 1.0
    return e


def _nsa_kernel(slopes_ref, q_ref, ks_ref, vs_ref, kw_ref, vw_ref, gt_ref, kvc_ref, ov_ref, e_ref, ng_ref,
                o_ref, ksb, vsb, kwb, vwb, bias_ref):
    T = ks_ref.shape[1]
    R, D = NSA_R, NSA_D
    N = R * TQ
    n_sel = T // SEL_BLOCK
    top_n = min(SEL_TOP, n_sel)
    g = pl.program_id(1)
    qi = pl.program_id(2)
    q0 = qi * TQ

    @pl.when(qi == 0)
    def _():
        cr = 256

        def body(r, c):
            r0 = pl.multiple_of(r * cr, cr)
            sl = pl.ds(r0, cr)
            kpos = r0 + lax.broadcasted_iota(jnp.int32, (cr, D), 0)
            lane_d = lax.broadcasted_iota(jnp.int32, (cr, D), 1)
            posf = jnp.where(lane_d == 0, kpos // SEL_BLOCK, jnp.where(lane_d == 1, kpos % SEL_BLOCK, 0))
            posf = posf.astype(F32).astype(BF16)
            ones = jnp.ones((cr, D), BF16)
            ksb[sl, 0:D] = ks_ref[0, sl, :].astype(BF16)
            ksb[sl, D:2 * D] = posf
            vsb[sl, 0:D] = vs_ref[0, sl, :].astype(BF16)
            vsb[sl, D:2 * D] = ones
            kwb[sl, 0:D] = kw_ref[0, sl, :].astype(BF16)
            kwb[sl, D:2 * D] = posf
            vwb[sl, 0:D] = vw_ref[0, sl, :].astype(BF16)
            vwb[sl, D:2 * D] = ones
            return c
        lax.fori_loop(0, T // cr, body, 0)

    def stack(x):
        return jnp.concatenate([x] * R, axis=0)

    q = q_ref[0] * (D ** -0.5)
    qs = jnp.concatenate([q[:, r * D:(r + 1) * D] for r in range(R)], axis=0).astype(BF16)
    slope = jnp.concatenate([jnp.full((TQ, 1), slopes_ref[g, r], F32) for r in range(R)], axis=0)
    lane_n = lax.broadcasted_iota(jnp.int32, (N, D), 1)
    slopef = jnp.where(lane_n == 0, slope * float(SEL_BLOCK), jnp.where(lane_n == 1, slope, 0.0))
    q_aug = jnp.concatenate([qs, slopef.astype(BF16)], axis=1)
    t1 = q0 + lax.broadcasted_iota(jnp.int32, (TQ, 1), 0)
    trow = stack(t1)

    kc = kvc_ref[0, 0, 0].astype(BF16)
    vc = kvc_ref[0, 0, 1].astype(BF16)
    ncmp = kc.shape[0]
    cend = lax.broadcasted_iota(jnp.int32, (1, ncmp), 1) * CMP_STRIDE + (CMP_BLOCK - 1)
    cend = jnp.where(cend < T, cend, 4 * T)
    valid_c = trow >= cend
    s = lax.dot_general(qs, kc, (((1,), (1,)), ((), ())), preferred_element_type=F32)
    s = s + slope * cend.astype(F32)
    s = jnp.where(valid_c, s, NEG)
    mx = jnp.max(s, axis=-1, keepdims=True)
    e = jnp.where(valid_c, jnp.exp(s - mx), 0.0)
    l = jnp.sum(e, axis=-1, keepdims=True)
    p_cmp = e / jnp.where(l > 0.0, l, 1.0)
    o_cmp = jnp.dot(p_cmp.astype(BF16), vc, preferred_element_type=F32)

    psum = p_cmp[0:TQ]
    for r in range(1, R):
        psum = psum + p_cmp[r * TQ:(r + 1) * TQ]
    ph = psum.astype(BF16)
    plo = (psum - ph.astype(F32)).astype(BF16)
    ov = ov_ref[...]
    imp = jnp.dot(ph, ov, preferred_element_type=F32) + jnp.dot(plo, ov, preferred_element_type=F32)

    blk = lax.broadcasted_iota(jnp.int32, (TQ, LANE), 1)
    cur = t1 // SEL_BLOCK
    forced = (blk == 0) | (blk == cur) | (blk == cur - 1)
    score = jnp.where(forced, FORCED_SCORE, jnp.where(blk <= cur, imp, -FORCED_SCORE))
    score = jnp.where(blk < n_sel, score, -4.0 * FORCED_SCORE)
    rank = jnp.zeros((TQ, LANE), F32)
    for kk in range(n_sel):
        sk = score[:, kk:kk + 1]
        beats = jnp.where(sk > score, 1.0, jnp.where(sk == score, jnp.where(blk > kk, 1.0, 0.0), 0.0))
        rank = rank + beats
    selmask = jnp.where(rank < float(top_n), 1.0, 0.0).astype(BF16)
    for c in range(T // KC):
        selk = jnp.dot(selmask, e_ref[:, c * KC:(c + 1) * KC], preferred_element_type=F32)
        kpos = c * KC + lax.broadcasted_iota(jnp.int32, (TQ, KC), 1)
        bias_ref[c] = jnp.where(kpos <= t1, jnp.where(selk > 0.5, 0.0, NEG), NEG)

    def sel_body(c, carry):
        m, acc = carry
        k0 = pl.multiple_of(c * KC, KC)
        s1 = lax.dot_general(q_aug, ksb[pl.ds(k0, KC), :], (((1,), (1,)), ((), ())), preferred_element_type=F32)
        bias = bias_ref[c]
        sm = jnp.concatenate([s1[r * TQ:(r + 1) * TQ] + bias for r in range(R)], axis=0)
        m_new = jnp.maximum(m, jnp.max(sm, axis=-1, keepdims=True))
        p = jnp.exp(sm - m_new)
        acc = jnp.exp(m - m_new) * acc + jnp.dot(p.astype(BF16), vsb[pl.ds(k0, KC), :], preferred_element_type=F32)
        return m_new, acc

    n_ch = (q0 + TQ - 1) // KC + 1
    _, acc_sel = lax.fori_loop(0, n_ch, sel_body, (jnp.full((N, 1), NEG, F32), jnp.zeros((N, 2 * D), F32)))
    o_sel = acc_sel[:, 0:D] / acc_sel[:, D:D + 1]

    start = pl.multiple_of(jnp.maximum(q0 - WINDOW, 0), TQ)
    dist = (t1 - start) - lax.broadcasted_iota(jnp.int32, (TQ, WSPAN), 1)
    bias_w = jnp.where(dist >= 0, jnp.where(dist < WINDOW, 0.0, NEG), NEG)
    sw = lax.dot_general(q_aug, kwb[pl.ds(start, WSPAN), :], (((1,), (1,)), ((), ())), preferred_element_type=F32)
    sw = jnp.concatenate([sw[r * TQ:(r + 1) * TQ] + bias_w for r in range(R)], axis=0)
    pw = jnp.exp(sw - jnp.max(sw, axis=-1, keepdims=True))
    acc_w = jnp.dot(pw.astype(BF16), vwb[pl.ds(start, WSPAN), :], preferred_element_type=F32)
    o_win = acc_w[:, 0:D] / acc_w[:, D:D + 1]

    sig = jax.nn.sigmoid(gt_ref[0])
    lane = lax.broadcasted_iota(jnp.int32, (TQ, LANE), 1)

    def gate_col(br):
        cols = []
        for r in range(R):
            cidx = (g * R + r) * 3 + br
            cols.append(jnp.sum(jnp.where(lane == cidx, sig, 0.0), axis=-1, keepdims=True))
        return jnp.concatenate(cols, axis=0)

    o = gate_col(0) * o_cmp + gate_col(1) * o_sel + gate_col(2) * o_win
    y = _rms_rows(o, ng_ref[...])
    for r in range(R):
        o_ref[0, :, r * D:(r + 1) * D] = y[r * TQ:(r + 1) * TQ].astype(o_ref.dtype)


def _nsa(proj3, kvc, slopes, ov, ex, ng):
    B, T, _ = proj3.shape
    ncmp = T // CMP_STRIDE

    def kv(cb):
        return pl.BlockSpec((1, T, LANE), lambda b, g, qi, s, cb=cb: (b, 0, cb + g))

    grid_spec = pltpu.PrefetchScalarGridSpec(
        num_scalar_prefetch=1,
        grid=(B, NSA_G, T // TQ),
        in_specs=[
            pl.BlockSpec((1, TQ, NSA_R * NSA_D), lambda b, g, qi, s: (b, qi, CB_NQ // NSA_R + g)),
            kv(CB_KSL), kv(CB_VSL), kv(CB_KWN), kv(CB_VWN),
            pl.BlockSpec((1, TQ, LANE), lambda b, g, qi, s: (b, qi, CB_GATE)),
            pl.BlockSpec((1, 1, 2, ncmp, NSA_D), lambda b, g, qi, s: (b, g, 0, 0, 0)),
            pl.BlockSpec((ncmp, LANE), lambda b, g, qi, s: (0, 0)),
            pl.BlockSpec((LANE, T), lambda b, g, qi, s: (0, 0)),
            pl.BlockSpec((1, NSA_D), lambda b, g, qi, s: (0, 0)),
        ],
        out_specs=pl.BlockSpec((1, TQ, NSA_R * NSA_D), lambda b, g, qi, s: (b, qi, g)),
        scratch_shapes=[pltpu.VMEM((T, 2 * NSA_D), BF16)] * 4 + [pltpu.VMEM((T // KC, TQ, KC), F32)],
    )
    return pl.pallas_call(
        _nsa_kernel,
        grid_spec=grid_spec,
        out_shape=jax.ShapeDtypeStruct((B, T, NSA_WIDTH), BF16),
        compiler_params=_cparams(("parallel", "parallel", "arbitrary")),
        name="nsa_attention",
    )(slopes, proj3, proj3, proj3, proj3, proj3, proj3, kvc, ov, ex, ng)


def _outproj_kernel(hg_ref, ns_ref, x_ref, wo_ref, g2_ref, wr_ref, br_ref,
                    h1_ref, xs_ref, idx_ref, w_ref, xn_ref):
    h1_ref[...] = (x_ref[...]
                   + jnp.dot(hg_ref[...], wo_ref[0:HG_WIDTH, :], preferred_element_type=F32)
                   + jnp.dot(ns_ref[...], wo_ref[HG_WIDTH:, :], preferred_element_type=F32))

    def body(r, c):
        r0 = pl.multiple_of(r * NORM_ROWS, NORM_ROWS)
        xn = _rms_rows(h1_ref[pl.ds(r0, NORM_ROWS), :], g2_ref[...])
        xn_ref[pl.ds(r0, NORM_ROWS), :] = xn
        for s in range(SEG):
            xs_ref[pl.ds(r0 * SEG + s, NORM_ROWS, stride=SEG), :] = xn[:, s * LANE:(s + 1) * LANE]
        return c
    lax.fori_loop(0, TM_OUT // NORM_ROWS, body, 0)

    logits = jnp.dot(xn_ref[...], wr_ref[...], preferred_element_type=F32) + br_ref[...]
    lane = lax.broadcasted_iota(jnp.int32, logits.shape, 1)
    lg = jnp.where(lane < N_EXPERTS, logits, -jnp.inf)
    vals, idxs = [], []
    for _ in range(TOP_K):
        mx = jnp.max(lg, axis=-1, keepdims=True)
        ix = jnp.min(jnp.where(lg == mx, lane, LANE), axis=-1, keepdims=True)
        vals.append(mx)
        idxs.append(ix)
        lg = jnp.where(lane == ix, -jnp.inf, lg)
    es = [jnp.exp(v - vals[0]) for v in vals]
    den = es[0]
    for e in es[1:]:
        den = den + e
    idx_out = jnp.zeros(logits.shape, jnp.int32)
    w_out = jnp.zeros(logits.shape, F32)
    for k in range(TOP_K):
        idx_out = jnp.where(lane == k, idxs[k], idx_out)
        w_out = jnp.where(lane == k, es[k] / den, w_out)
    idx_ref[...] = idx_out
    w_ref[...] = w_out


def _outproj(y_hg, y_nsa, x2, wo, g2, wr_pad, br_pad):
    n_tok = x2.shape[0]
    row = lambda w: pl.BlockSpec((TM_OUT, w), lambda i: (i, 0))
    full = lambda a, b: pl.BlockSpec((a, b), lambda i: (0, 0))
    return pl.pallas_call(
        _outproj_kernel,
        grid=(n_tok // TM_OUT,),
        in_specs=[row(HG_WIDTH), row(NSA_WIDTH), row(D_MODEL), full(D_MODEL, D_MODEL), full(1, D_MODEL),
                  full(D_MODEL, LANE), full(1, LANE)],
        out_specs=[row(D_MODEL), pl.BlockSpec((TM_OUT * SEG, LANE), lambda i: (i, 0)), row(LANE), row(LANE)],
        out_shape=[jax.ShapeDtypeStruct((n_tok, D_MODEL), F32), jax.ShapeDtypeStruct((n_tok * SEG, LANE), F32),
                   jax.ShapeDtypeStruct((n_tok, LANE), jnp.int32), jax.ShapeDtypeStruct((n_tok, LANE), F32)],
        scratch_shapes=[pltpu.VMEM((TM_OUT, D_MODEL), F32)],
        compiler_params=_cparams(("parallel",)),
        name="outproj_router",
    )(y_hg, y_nsa, x2, wo, g2, wr_pad, br_pad)


def _moe_kernel(e_ref, sbase_ref, rows_ref, nu_ref, stok_ref,
                xs_hbm, wg_ref, wu_ref, wd_ref, bg_ref, bu_ref, bd_ref,
                y_ref, xstage, xg, act, sem):
    i = pl.program_id(0)
    s = pl.program_id(1)
    nu = nu_ref[0]
    n_asg = stok_ref.shape[0]
    nq = (rows_ref[i] + RQ_MOE - 1) // RQ_MOE
    unroll = 8

    def row_copy(tok, r):
        return pltpu.make_async_copy(xs_hbm.at[pl.ds(tok * SEG, SEG)], xstage.at[pl.ds(r * SEG, SEG)], sem.at[0])

    def issue(blk):
        base = sbase_ref[blk]
        nrow = (rows_ref[blk] + RQ_MOE - 1) // RQ_MOE * RQ_MOE

        def body(r8, c):
            for u in range(unroll):
                r = r8 * unroll + u
                row_copy(stok_ref[jnp.minimum(base + r, n_asg - 1)], r).start()
            return c
        lax.fori_loop(0, nrow // unroll, body, 0)

    @pl.when(s == 0)
    def _():
        @pl.when(i == 0)
        def _():
            issue(0)

        @pl.when(i < nu)
        def _():
            nrow = nq * RQ_MOE
            pltpu.make_async_copy(xs_hbm.at[pl.ds(0, nrow * SEG)], xstage.at[pl.ds(0, nrow * SEG)], sem.at[0]).wait()

            def cbody(r, c):
                r0 = pl.multiple_of(r * 64, 64)
                for sg in range(SEG):
                    xg[pl.ds(r0, 64), sg * LANE:(sg + 1) * LANE] = (
                        xstage[pl.ds(r0 * SEG + sg, 64, stride=SEG), :].astype(BF16))
                return c
            lax.fori_loop(0, nrow // 64, cbody, 0)

        @pl.when(i + 1 < nu)
        def _():
            issue(i + 1)

    def for_sub_blocks(fn):
        per = TS_MOE // RQ_MOE
        n_full = nq // per
        rem = nq % per

        def body(j, c):
            fn(pl.multiple_of(j * TS_MOE, TS_MOE), TS_MOE)
            return c
        lax.fori_loop(0, n_full, body, 0)
        r0 = n_full * TS_MOE
        size = TS_MOE // 2
        while size >= RQ_MOE:
            q = size // RQ_MOE

            @pl.when((rem & q) != 0)
            def _(r0=r0, size=size):
                fn(pl.multiple_of(r0, RQ_MOE), size)
            r0 = r0 + (rem & q) * RQ_MOE
            size //= 2

    @pl.when((s < NF_MOE) & (i < nu))
    def _():
        bg = bg_ref[0]
        bu = bu_ref[0]

        def phase_a(r0, size):
            rs = pl.ds(r0, size)
            x = xg[rs, :]
            gate = jnp.dot(x, wg_ref[0].astype(BF16), preferred_element_type=F32) + bg
            up = jnp.dot(x, wu_ref[0].astype(BF16), preferred_element_type=F32) + bu
            gate = jnp.minimum(gate, SWIGLU_LIMIT)
            up = jnp.clip(up, -SWIGLU_LIMIT, SWIGLU_LIMIT)
            a = gate * jax.nn.sigmoid(SWIGLU_ALPHA * gate) * (up + 1.0)
            act[s, rs, :] = a.astype(BF16)
        for_sub_blocks(phase_a)

    @pl.when((s >= NF_MOE) & (i < nu))
    def _():
        bd = bd_ref[0]

        def phase_b(r0, size):
            rs = pl.ds(r0, size)
            a = jnp.concatenate([act[f, rs, :] for f in range(NF_MOE)], axis=1)
            y_ref[rs, :] = jnp.dot(a, wd_ref[0].astype(BF16), preferred_element_type=F32) + bd
        for_sub_blocks(phase_b)

        def zfill(j, c):
            y_ref[pl.ds(pl.multiple_of(j * RQ_MOE, RQ_MOE), RQ_MOE), :] = jnp.zeros((RQ_MOE, FCB_MOE), F32)
            return c
        lax.fori_loop(nq, SB_MOE // RQ_MOE, zfill, 0)

    @pl.when((s >= NF_MOE) & (i >= nu))
    def _():
        y_ref[...] = jnp.zeros_like(y_ref)


def _moe(sup_e, sup_base, sup_rows, n_used, stok, xs, wgu, wd, bgu, bd):
    ns = sup_e.shape[0]

    def e_of(i, e, nu):
        return e[jnp.where(i < nu[0], i, nu[0] - 1)]

    def f_of(i, s, nu):
        return jnp.where(i < nu[0], jnp.minimum(s, NF_MOE - 1), NF_MOE - 1)

    def n_of(i, s, nu):
        return jnp.where(i < nu[0], jnp.maximum(s - NF_MOE, 0), NN_MOE - 1)

    def wg_map(i, s, e, sb, rw, nu, tk):
        return (e_of(i, e, nu), 0, f_of(i, s, nu))

    def wu_map(i, s, e, sb, rw, nu, tk):
        return (e_of(i, e, nu), 0, NF_MOE + f_of(i, s, nu))

    def wd_map(i, s, e, sb, rw, nu, tk):
        return (e_of(i, e, nu), 0, n_of(i, s, nu))

    def y_map(i, s, e, sb, rw, nu, tk):
        return (i, jnp.maximum(s - NF_MOE, 0))

    grid_spec = pltpu.PrefetchScalarGridSpec(
        num_scalar_prefetch=5,
        grid=(ns, NF_MOE + NN_MOE),
        in_specs=[
            pl.BlockSpec(memory_space=pl.ANY),
            pl.BlockSpec((1, D_MODEL, FC_MOE), wg_map),
            pl.BlockSpec((1, D_MODEL, FC_MOE), wu_map),
            pl.BlockSpec((1, D_FF, FCB_MOE), wd_map),
            pl.BlockSpec((1, 1, FC_MOE), wg_map),
            pl.BlockSpec((1, 1, FC_MOE), wu_map),
            pl.BlockSpec((1, 1, FCB_MOE), wd_map),
        ],
        out_specs=pl.BlockSpec((SB_MOE, FCB_MOE), y_map),
        scratch_shapes=[pltpu.VMEM((SB_MOE * SEG, LANE), F32), pltpu.VMEM((SB_MOE, D_MODEL), BF16),
                        pltpu.VMEM((NF_MOE, SB_MOE, FC_MOE), BF16), pltpu.SemaphoreType.DMA((1,))],
    )
    return pl.pallas_call(
        _moe_kernel,
        grid_spec=grid_spec,
        out_shape=jax.ShapeDtypeStruct((ns * SB_MOE, D_MODEL), F32),
        compiler_params=_cparams(("arbitrary", "arbitrary")),
        name="moe_experts",
    )(sup_e, sup_base, sup_rows, n_used, stok, xs, wgu, wgu, wd, bgu, bgu, bd)


def _combine_kernel(pos_ref, y_hbm, h1_ref, w_ref, g_ref, o_ref, ybuf, sem):
    i = pl.program_id(0)
    n = pl.num_programs(0)
    slot = i % 2

    unroll = 4

    def row_copy(src, sl, k, r):
        return pltpu.make_async_copy(y_hbm.at[src], ybuf.at[sl, k * TC_CMB + r], sem.at[sl])

    def issue(blk, sl):
        def body(r4, c):
            for u in range(unroll):
                r = r4 * unroll + u
                for k in range(TOP_K):
                    row_copy(pos_ref[(blk * TC_CMB + r) * TOP_K + k], sl, k, r).start()
            return c
        lax.fori_loop(0, TC_CMB // unroll, body, 0)

    @pl.when(i == 0)
    def _():
        issue(0, 0)

    pltpu.make_async_copy(y_hbm.at[pl.ds(0, TOP_K * TC_CMB)], ybuf.at[slot], sem.at[slot]).wait()

    @pl.when(i + 1 < n)
    def _():
        issue(i + 1, 1 - slot)

    w = w_ref[...]
    h = h1_ref[...]
    for k in range(TOP_K):
        h = h + w[:, k:k + 1] * ybuf[slot, k * TC_CMB:(k + 1) * TC_CMB, :]
    o_ref[...] = _rms_rows(h, g_ref[...])


def _combine(pos, y_sorted, h1, top_w, g):
    n_tok = h1.shape[0]
    grid_spec = pltpu.PrefetchScalarGridSpec(
        num_scalar_prefetch=1,
        grid=(n_tok // TC_CMB,),
        in_specs=[
            pl.BlockSpec(memory_space=pl.ANY),
            pl.BlockSpec((TC_CMB, D_MODEL), lambda i, p: (i, 0)),
            pl.BlockSpec((TC_CMB, LANE), lambda i, p: (i, 0)),
            pl.BlockSpec((1, D_MODEL), lambda i, p: (0, 0)),
        ],
        out_specs=pl.BlockSpec((TC_CMB, D_MODEL), lambda i, p: (i, 0)),
        scratch_shapes=[pltpu.VMEM((2, TOP_K * TC_CMB, D_MODEL), F32), pltpu.SemaphoreType.DMA((2,))],
    )
    return pl.pallas_call(
        _combine_kernel,
        grid_spec=grid_spec,
        out_shape=jax.ShapeDtypeStruct((n_tok, D_MODEL), F32),
        compiler_params=_cparams(("arbitrary",)),
        name="moe_combine",
    )(pos, y_sorted, h1, top_w, g)


def _dispatch(top_idx):
    n_tok = top_idx.shape[0]
    n_asg = n_tok * TOP_K
    i32 = jnp.int32
    flat_e = top_idx.reshape(n_asg)
    order = jnp.argsort(flat_e).astype(i32)
    inv = jnp.argsort(order).astype(i32)
    stok = order // TOP_K
    eids = jnp.arange(N_EXPERTS, dtype=i32)
    counts = jnp.sum((flat_e[:, None] == eids[None, :]).astype(i32), axis=0)
    start = jnp.cumsum(counts) - counts
    nsup = (counts + SB_MOE - 1) // SB_MOE
    sup_end = jnp.cumsum(nsup)
    sup_start = sup_end - nsup
    per = jnp.where(nsup > 0, (counts + nsup * RQ_MOE - 1) // jnp.maximum(nsup * RQ_MOE, 1) * RQ_MOE, 0)
    ns = -(-n_asg // SB_MOE) + N_EXPERTS
    ii = jnp.arange(ns, dtype=i32)
    sup_e = jnp.minimum(jnp.sum((sup_end[None, :] <= ii[:, None]).astype(i32), axis=1), N_EXPERTS - 1)
    k_i = ii - sup_start[sup_e]
    sup_base = (start[sup_e] + k_i * per[sup_e]).astype(i32)
    sup_rows = jnp.clip(counts[sup_e] - k_i * per[sup_e], 0, per[sup_e]).astype(i32)
    n_used = sup_end[-1:].astype(i32)
    rank = inv - start[flat_e]
    pe = jnp.maximum(per[flat_e], 1)
    pos = ((sup_start[flat_e] + rank // pe) * SB_MOE + rank % pe).astype(i32)
    return sup_e.astype(i32), sup_base, sup_rows, n_used, stok, pos


def _alibi_slopes():
    h = np.arange(1, NSA_HEADS + 1, dtype=np.float32)
    return np.power(np.float32(2.0), -8.0 * h / NSA_HEADS).astype(np.float32).reshape(NSA_G, NSA_R)


def kernel(x, norm1_g, w_in, hg_lb_logits, hg_norm_g, cmp_pos, cmp_w1, cmp_w2, nsa_norm_g, w_out,
           norm2_g, w_router, b_router, w_gate_up, b_gate_up, w_down, b_down, final_norm_g):
    B, T, D = x.shape
    n_tok = B * T
    l = 0
    x2 = x.reshape(n_tok, D)

    w_in_pad = jnp.pad(w_in[l], ((0, 0), (0, IN_COLS_PAD - IN_COLS))).astype(BF16)
    lb_all = jnp.cumsum(jax.nn.softmax(hg_lb_logits.astype(F32), axis=0), axis=0)
    lb = lb_all[l].reshape(1, HG_WIDTH)
    w1r = cmp_w1[l].reshape(2, CMP_BLOCK, NSA_D, NSA_D).astype(BF16)
    w2b = cmp_w2[l].astype(BF16)
    wo = w_out[l].astype(BF16)
    wr_pad = jnp.pad(w_router[l], ((0, 0), (0, LANE - N_EXPERTS)))
    br_pad = jnp.pad(b_router[l], (0, LANE - N_EXPERTS)).reshape(1, LANE)
    bgu = b_gate_up[l].reshape(N_EXPERTS, 1, 2 * D_FF)
    bd = b_down[l].reshape(N_EXPERTS, 1, D_MODEL)

    proj = _inproj(x2, norm1_g[l].reshape(1, D), w_in_pad)
    proj3 = proj.reshape(B, T, IN_COLS_PAD)
    y_hg = _hgrn(proj3, lb, hg_norm_g[l].reshape(1, HG_DK))
    kvc = _compress(proj3, cmp_pos[l], w1r, w2b)
    y_nsa = _nsa(proj3, kvc, jnp.asarray(_alibi_slopes()), jnp.asarray(_overlap_matrix(T), BF16),
                 jnp.asarray(_expand_matrix(T), BF16), nsa_norm_g[l].reshape(1, NSA_D))
    h1, xs, top_idx, top_w = _outproj(y_hg.reshape(n_tok, HG_WIDTH), y_nsa.reshape(n_tok, NSA_WIDTH), x2, wo,
                                      norm2_g[l].reshape(1, D), wr_pad, br_pad)
    sup_e, sup_base, sup_rows, n_used, stok, pos = _dispatch(top_idx[:, :TOP_K])
    y_sorted = _moe(sup_e, sup_base, sup_rows, n_used, stok, xs, w_gate_up[l], w_down[l], bgu, bd)
    out = _combine(pos, y_sorted, h1, top_w, final_norm_g.reshape(1, D))
    return out.reshape(B, T, D)
```

```python
import functools

import numpy as np
import jax
import jax.numpy as jnp
from jax import lax
from jax.experimental import pallas as pl
from jax.experimental.pallas import tpu as pltpu

F32 = jnp.float32
BF16 = jnp.bfloat16

D_MODEL = 2048
HG_WIDTH = 1024
HG_DK = 128
HG_HEADS = 8
HG_CHUNK = 64
NSA_WIDTH = 1024
NSA_D = 128
NSA_HEADS = 8
NSA_G = 2
NSA_R = 4
KV_WIDTH = NSA_G * NSA_D
CMP_BLOCK = 32
CMP_STRIDE = 16
SEL_BLOCK = 64
SEL_TOP = 16
WINDOW = 512
N_EXPERTS = 32
TOP_K = 4
D_FF = 2048
SWIGLU_LIMIT = 7.0
SWIGLU_ALPHA = 1.702
EPS = 1e-6
NEG = -1e30
FORCED_SCORE = 1e4
IN_COLS = 4 * HG_WIDTH + NSA_WIDTH + 6 * KV_WIDTH + 3 * NSA_HEADS

LANE = 128
CB_HQ, CB_HF, CB_HI, CB_HG = 0, 8, 16, 24
CB_NQ = 32
CB_KCM, CB_VCM, CB_KSL, CB_VSL, CB_KWN, CB_VWN = 40, 42, 44, 46, 48, 50
CB_GATE = 52

TM_IN = 1024
TN_IN = 768
IN_COLS_PAD = 9 * TN_IN
NORM_ROWS = 32
HG_TT = 256
TQ = 128
KC = 512
NSA_SPLIT = 2
WSPAN = WINDOW + TQ
TM_OUT = 512
SEG = D_MODEL // LANE
SB_MOE = 1536
TS_MOE = 512
RQ_MOE = 128
FC_MOE = 256
FCB_MOE = 512
NF_MOE = D_FF // FC_MOE
NN_MOE = D_MODEL // FCB_MOE
TC_CMB = 128
VMEM_LIMIT = 56 * 1024 * 1024


def _cparams(sem):
    return pltpu.CompilerParams(dimension_semantics=sem, vmem_limit_bytes=VMEM_LIMIT)


def _rms_rows(x, g):
    ms = jnp.mean(x * x, axis=-1, keepdims=True)
    return x * lax.rsqrt(ms + EPS) * g


def _inproj_kernel(x_ref, g_ref, w_ref, o_ref, u_ref):
    @pl.when(pl.program_id(1) == 0)
    def _():
        def body(r, c):
            sl = pl.ds(pl.multiple_of(r * NORM_ROWS, NORM_ROWS), NORM_ROWS)
            u_ref[sl, :] = _rms_rows(x_ref[sl, :], g_ref[...]).astype(BF16)
            return c
        lax.fori_loop(0, TM_IN // NORM_ROWS, body, 0)

    o_ref[...] = jnp.dot(u_ref[...], w_ref[...], preferred_element_type=F32)


def _inproj(x2, g, w_pad):
    n_tok = x2.shape[0]
    return pl.pallas_call(
        _inproj_kernel,
        grid=(n_tok // TM_IN, IN_COLS_PAD // TN_IN),
        in_specs=[
            pl.BlockSpec((TM_IN, D_MODEL), lambda i, j: (i, 0)),
            pl.BlockSpec((1, D_MODEL), lambda i, j: (0, 0)),
            pl.BlockSpec((D_MODEL, TN_IN), lambda i, j: (0, j)),
        ],
        out_specs=pl.BlockSpec((TM_IN, TN_IN), lambda i, j: (i, j)),
        out_shape=jax.ShapeDtypeStruct((n_tok, IN_COLS_PAD), F32),
        scratch_shapes=[pltpu.VMEM((TM_IN, D_MODEL), BF16)],
        compiler_params=_cparams(("parallel", "arbitrary")),
        name="inproj",
    )(x2, g, w_pad)


def _hgrn_kernel(q_ref, f_ref, i_ref, g_ref, lb_ref, ng_ref, o_ref, st_ref, qd_ref, kd_ref, kr_ref, el_ref):
    C = HG_CHUNK
    nch = HG_TT // C
    ng = ng_ref[...]
    row = lax.broadcasted_iota(jnp.int32, (C, C), 0)
    col = lax.broadcasted_iota(jnp.int32, (C, C), 1)
    tril = row >= col
    tri_bf = jnp.where(tril, 1.0, 0.0).astype(BF16)
    lb = lb_ref[...]

    @pl.when(pl.program_id(1) == 0)
    def _():
        st_ref[...] = jnp.zeros_like(st_ref)

    for c in range(nch):
        sl = slice(c * C, (c + 1) * C)
        f = lb + (1.0 - lb) * jax.nn.sigmoid(f_ref[0, sl, :])
        logf = jnp.log(f)
        k = 1.0 - f
        hi = logf.astype(BF16)
        lo = (logf - hi.astype(F32)).astype(BF16)
        b = jnp.dot(tri_bf, hi, preferred_element_type=F32) + jnp.dot(tri_bf, lo, preferred_element_type=F32)
        e_last = jnp.exp(b[C - 1:C, :])
        k_dec = k * jnp.exp(-b)
        qd_ref[sl, :] = (jax.nn.silu(q_ref[0, sl, :]) * jnp.exp(b)).astype(BF16)
        kd_ref[sl, :] = k_dec.astype(BF16)
        kr_ref[sl, :] = (k_dec * e_last).astype(BF16)
        el_ref[c] = e_last

    for h in range(HG_HEADS):
        hs = slice(h * HG_DK, (h + 1) * HG_DK)
        st = st_ref[h]
        for c in range(nch):
            sl = slice(c * C, (c + 1) * C)
            q_dec = qd_ref[sl, hs]
            v_bf = i_ref[0, sl, hs].astype(BF16)
            a = lax.dot_general(q_dec, kd_ref[sl, hs], (((1,), (1,)), ((), ())), preferred_element_type=F32)
            a = jnp.where(tril, a, 0.0).astype(BF16)
            o = jnp.dot(a, v_bf, preferred_element_type=F32)
            o = o + lax.dot_general(q_dec, st.astype(BF16), (((1,), (1,)), ((), ())), preferred_element_type=F32)
            st = st * el_ref[c][:, hs] + lax.dot_general(v_bf, kr_ref[sl, hs], (((0,), (0,)), ((), ())),
                                                         preferred_element_type=F32)
            y = _rms_rows(o, ng) * jax.nn.silu(g_ref[0, sl, hs])
            o_ref[0, sl, hs] = y.astype(o_ref.dtype)
        st_ref[h] = st


def _hgrn(proj3, lb, ng):
    B, T, _ = proj3.shape

    def col(cb):
        return pl.BlockSpec((1, HG_TT, HG_WIDTH), lambda b, t, cb=cb: (b, t, cb // HG_HEADS))

    return pl.pallas_call(
        _hgrn_kernel,
        grid=(B, T // HG_TT),
        in_specs=[col(CB_HQ), col(CB_HF), col(CB_HI), col(CB_HG),
                  pl.BlockSpec((1, HG_WIDTH), lambda b, t: (0, 0)),
                  pl.BlockSpec((1, LANE), lambda b, t: (0, 0))],
        out_specs=pl.BlockSpec((1, HG_TT, HG_WIDTH), lambda b, t: (b, t, 0)),
        out_shape=jax.ShapeDtypeStruct((B, T, HG_WIDTH), BF16),
        scratch_shapes=[pltpu.VMEM((HG_HEADS, HG_DK, HG_DK), F32)] + [pltpu.VMEM((HG_TT, HG_WIDTH), BF16)] * 3
        + [pltpu.VMEM((HG_TT // HG_CHUNK, 1, HG_WIDTH), F32)],
        compiler_params=_cparams(("parallel", "arbitrary")),
        name="hgrn2",
    )(proj3, proj3, proj3, proj3, lb, ng)


def _gelu_tanh(x):
    return 0.5 * x * (1.0 + jnp.tanh(0.7978845608028654 * (x + 0.044715 * (x * x * x))))


def _compress_kernel(x_ref, pos_ref, w1_ref, w2_ref, o_ref):
    T = x_ref.shape[1]
    ng = T // CMP_STRIDE
    acc_a = jnp.zeros((ng, NSA_D), F32)
    acc_b = jnp.zeros((ng, NSA_D), F32)
    for r in range(CMP_STRIDE):
        xr = x_ref[0, pl.ds(r, ng, stride=CMP_STRIDE), :]
        xa = (xr + pos_ref[0, r:r + 1, :]).astype(BF16)
        xb = (xr + pos_ref[0, CMP_STRIDE + r:CMP_STRIDE + r + 1, :]).astype(BF16)
        acc_a = acc_a + jnp.dot(xa, w1_ref[0, r], preferred_element_type=F32)
        acc_b = acc_b + jnp.dot(xb, w1_ref[0, CMP_STRIDE + r], preferred_element_type=F32)
    hdn = _gelu_tanh(acc_a + pltpu.roll(acc_b, ng - 1, 0))
    o_ref[0, 0, 0] = jnp.dot(hdn.astype(BF16), w2_ref[0], preferred_element_type=F32)


def _compress(proj3, pos, w1r, w2b):
    B, T, _ = proj3.shape
    ng = T // CMP_STRIDE
    return pl.pallas_call(
        _compress_kernel,
        grid=(B, NSA_G, 2),
        in_specs=[
            pl.BlockSpec((1, T, LANE), lambda b, g, j: (b, 0, CB_KCM + 2 * j + g)),
            pl.BlockSpec((1, CMP_BLOCK, NSA_D), lambda b, g, j: (j, 0, 0)),
            pl.BlockSpec((1, CMP_BLOCK, NSA_D, NSA_D), lambda b, g, j: (j, 0, 0, 0)),
            pl.BlockSpec((1, NSA_D, NSA_D), lambda b, g, j: (j, 0, 0)),
        ],
        out_specs=pl.BlockSpec((1, 1, 1, ng, NSA_D), lambda b, g, j: (b, g, j, 0, 0)),
        out_shape=jax.ShapeDtypeStruct((B, NSA_G, 2, ng, NSA_D), F32),
        compiler_params=_cparams(("parallel", "parallel", "parallel")),
        name="nsa_compress",
    )(proj3, pos, w1r, w2b)


def _overlap_matrix(T):
    n_cmp = (T - CMP_BLOCK) // CMP_STRIDE + 1
    n_sel = T // SEL_BLOCK
    ratio_s = SEL_BLOCK // CMP_STRIDE
    ratio_c = CMP_BLOCK // CMP_STRIDE
    jj, mm, nn = np.meshgrid(np.arange(n_sel), np.arange(ratio_s), np.arange(ratio_c), indexing='ij')
    ii = ratio_s * jj + mm + nn - (ratio_c - 1)
    ok = (ii >= 0) & (ii < n_cmp)
    overlap = np.zeros((T // CMP_STRIDE, LANE), np.float32)
    np.add.at(overlap, (ii[ok], jj[ok]), 1.0)
    return overlap


MASK_OFF = -2.0 ** 100


def _expand_matrix(T):
    e = np.zeros((LANE, T), np.float32)
    e[np.arange(T) // SEL_BLOCK, np.arange(T)] = MASK_OFF
    return e


def _nsa_kernel(slopes_ref, q_ref, ks_ref, vs_ref, kw_ref, vw_ref, gt_ref, kvc_ref, ov_ref, e_ref, ng_ref,
                o_ref, ksb, vsb, kwb, vwb, bias_ref, s0_ref, s1_ref, p0_ref, p1_ref):
    T = ks_ref.shape[1]
    R, D = NSA_R, NSA_D
    N = R * TQ
    n_sel = T // SEL_BLOCK
    top_n = min(SEL_TOP, n_sel)
    g = pl.program_id(1)
    qi = pl.program_id(2)
    q0 = qi * TQ

    @pl.when(qi == 0)
    def _():
        cr = 256

        def body(r, c):
            r0 = pl.multiple_of(r * cr, cr)
            sl = pl.ds(r0, cr)
            kpos = r0 + lax.broadcasted_iota(jnp.int32, (cr, D), 0)
            lane_d = lax.broadcasted_iota(jnp.int32, (cr, D), 1)
            posf = jnp.where(lane_d == 0, kpos // SEL_BLOCK, jnp.where(lane_d == 1, kpos % SEL_BLOCK, 0))
            posf = posf.astype(F32).astype(BF16)
            ones = jnp.ones((cr, D), BF16)
            ksb[sl, 0:D] = ks_ref[0, sl, :].astype(BF16)
            ksb[sl, D:2 * D] = posf
            vsb[sl, 0:D] = vs_ref[0, sl, :].astype(BF16)
            vsb[sl, D:2 * D] = ones
            kwb[sl, 0:D] = kw_ref[0, sl, :].astype(BF16)
            kwb[sl, D:2 * D] = posf
            vwb[sl, 0:D] = vw_ref[0, sl, :].astype(BF16)
            vwb[sl, D:2 * D] = ones
            return c
        lax.fori_loop(0, T // cr, body, 0)
        bias_ref[T // KC] = jnp.full((TQ, KC), NEG, F32)

    def stack(x):
        return jnp.concatenate([x] * R, axis=0)

    q = q_ref[0] * (D ** -0.5)
    qs = jnp.concatenate([q[:, r * D:(r + 1) * D] for r in range(R)], axis=0).astype(BF16)
    slope = jnp.concatenate([jnp.full((TQ, 1), slopes_ref[g, r], F32) for r in range(R)], axis=0)
    lane_n = lax.broadcasted_iota(jnp.int32, (N, D), 1)
    slopef = jnp.where(lane_n == 0, slope * float(SEL_BLOCK), jnp.where(lane_n == 1, slope, 0.0))
    q_aug = jnp.concatenate([qs, slopef.astype(BF16)], axis=1)
    t1 = q0 + lax.broadcasted_iota(jnp.int32, (TQ, 1), 0)
    trow = stack(t1)

    kc = kvc_ref[0, 0, 0].astype(BF16)
    vc = kvc_ref[0, 0, 1].astype(BF16)
    ncmp = kc.shape[0]
    cend = lax.broadcasted_iota(jnp.int32, (1, ncmp), 1) * CMP_STRIDE + (CMP_BLOCK - 1)
    cend = jnp.where(cend < T, cend, 4 * T)
    valid_c = trow >= cend
    s = lax.dot_general(qs, kc, (((1,), (1,)), ((), ())), preferred_element_type=F32)
    s = s + slope * cend.astype(F32)
    s = jnp.where(valid_c, s, NEG)
    mx = jnp.max(s, axis=-1, keepdims=True)
    e = jnp.where(valid_c, jnp.exp(s - mx), 0.0)
    l = jnp.sum(e, axis=-1, keepdims=True)
    p_cmp = e / jnp.where(l > 0.0, l, 1.0)
    o_cmp = jnp.dot(p_cmp.astype(BF16), vc, preferred_element_type=F32)

    psum = p_cmp[0:TQ]
    for r in range(1, R):
        psum = psum + p_cmp[r * TQ:(r + 1) * TQ]
    ph = psum.astype(BF16)
    plo = (psum - ph.astype(F32)).astype(BF16)
    ov = ov_ref[...]
    imp = jnp.dot(ph, ov, preferred_element_type=F32) + jnp.dot(plo, ov, preferred_element_type=F32)

    blk = lax.broadcasted_iota(jnp.int32, (TQ, LANE), 1)
    cur = t1 // SEL_BLOCK
    forced = (blk == 0) | (blk == cur) | (blk == cur - 1)
    score = jnp.where(forced, FORCED_SCORE, jnp.where(blk <= cur, imp, -FORCED_SCORE))
    score = jnp.where(blk < n_sel, score, -4.0 * FORCED_SCORE)
    rank = jnp.zeros((TQ, LANE), F32)
    for kk in range(n_sel):
        sk = score[:, kk:kk + 1]
        beats = jnp.where(sk > score, 1.0, jnp.where(sk == score, jnp.where(blk > kk, 1.0, 0.0), 0.0))
        rank = rank + beats
    notsel = jnp.where(rank < float(top_n), 0.0, 1.0).astype(BF16)
    nchunks = T // KC
    for c in range(nchunks):
        bias_ref[c] = jnp.dot(notsel, e_ref[:, c * KC:(c + 1) * KC], preferred_element_type=F32)
    cd = q0 // KC
    kpos = cd * KC + lax.broadcasted_iota(jnp.int32, (TQ, KC), 1)
    bias_ref[cd] = bias_ref[cd] + jnp.where(kpos <= t1, 0.0, NEG)

    start = pl.multiple_of(jnp.maximum(q0 - WINDOW, 0), TQ)
    dist = (t1 - start) - lax.broadcasted_iota(jnp.int32, (TQ, WSPAN), 1)
    bias_w = jnp.where(dist >= 0, jnp.where(dist < WINDOW, 0.0, NEG), NEG)
    kw = kwb[pl.ds(start, WSPAN), :]
    vw = vwb[pl.ds(start, WSPAN), :]
    o_win_parts = []
    for hf in range(NSA_SPLIT):
        rows = slice(hf * N // NSA_SPLIT, (hf + 1) * N // NSA_SPLIT)
        sw = lax.dot_general(q_aug[rows], kw, (((1,), (1,)), ((), ())), preferred_element_type=F32)
        sw = jnp.concatenate([sw[r * TQ:(r + 1) * TQ] + bias_w for r in range(R // NSA_SPLIT)], axis=0)
        pw = jnp.exp(sw - jnp.max(sw, axis=-1, keepdims=True))
        acc_w = jnp.dot(pw.astype(BF16), vw, preferred_element_type=F32)
        o_win_parts.append(acc_w[:, 0:D] / acc_w[:, D:D + 1])
    o_win = jnp.concatenate(o_win_parts, axis=0)

    n_ch = cd + 1

    def scores(c):
        k0 = pl.multiple_of(jnp.minimum(c, nchunks - 1) * KC, KC)
        return lax.dot_general(q_aug, ksb[pl.ds(k0, KC), :], (((1,), (1,)), ((), ())), preferred_element_type=F32)

    def soft(s_ref, c, m):
        bias = bias_ref[jnp.where(c < n_ch, c, nchunks)]
        sm = jnp.concatenate([s_ref[r * TQ:(r + 1) * TQ, :] + bias for r in range(R)], axis=0)
        m_new = jnp.maximum(m, jnp.max(sm, axis=-1, keepdims=True))
        return m_new, jnp.exp(m - m_new), jnp.exp(sm - m_new).astype(BF16)

    def pv(p_ref, c, alpha, acc):
        k0 = pl.multiple_of(jnp.minimum(c, nchunks - 1) * KC, KC)
        return alpha * acc + jnp.dot(p_ref[...], vsb[pl.ds(k0, KC), :], preferred_element_type=F32)

    s0_ref[...] = scores(0)
    m1, alpha1, p = soft(s0_ref, 0, jnp.full((N, 1), NEG, F32))
    p0_ref[...] = p
    s1_ref[...] = scores(1)

    def sel_body(jj, carry):
        m, alpha, acc = carry
        j = 2 * jj
        acc = pv(p0_ref, j, alpha, acc)
        m, alpha, p = soft(s1_ref, j + 1, m)
        p1_ref[...] = p
        s0_ref[...] = scores(j + 2)
        acc = pv(p1_ref, j + 1, alpha, acc)
        m, alpha, p = soft(s0_ref, j + 2, m)
        p0_ref[...] = p
        s1_ref[...] = scores(j + 3)
        return m, alpha, acc

    _, _, acc_sel = lax.fori_loop(0, (n_ch + 1) // 2, sel_body, (m1, alpha1, jnp.zeros((N, 2 * D), F32)))
    o_sel = acc_sel[:, 0:D] / acc_sel[:, D:D + 1]

    sig = jax.nn.sigmoid(gt_ref[0])
    lane = lax.broadcasted_iota(jnp.int32, (TQ, LANE), 1)

    def gate_col(br):
        cols = []
        for r in range(R):
            cidx = (g * R + r) * 3 + br
            cols.append(jnp.sum(jnp.where(lane == cidx, sig, 0.0), axis=-1, keepdims=True))
        return jnp.concatenate(cols, axis=0)

    o = gate_col(0) * o_cmp + gate_col(1) * o_sel + gate_col(2) * o_win
    y = _rms_rows(o, ng_ref[...])
    for r in range(R):
        o_ref[0, :, r * D:(r + 1) * D] = y[r * TQ:(r + 1) * TQ].astype(o_ref.dtype)


def _nsa(proj3, kvc, slopes, ov, ex, ng):
    B, T, _ = proj3.shape
    ncmp = T // CMP_STRIDE

    def kv(cb):
        return pl.BlockSpec((1, T, LANE), lambda b, g, qi, s, cb=cb: (b, 0, cb + g))

    grid_spec = pltpu.PrefetchScalarGridSpec(
        num_scalar_prefetch=1,
        grid=(B, NSA_G, T // TQ),
        in_specs=[
            pl.BlockSpec((1, TQ, NSA_R * NSA_D), lambda b, g, qi, s: (b, qi, CB_NQ // NSA_R + g)),
            kv(CB_KSL), kv(CB_VSL), kv(CB_KWN), kv(CB_VWN),
            pl.BlockSpec((1, TQ, LANE), lambda b, g, qi, s: (b, qi, CB_GATE)),
            pl.BlockSpec((1, 1, 2, ncmp, NSA_D), lambda b, g, qi, s: (b, g, 0, 0, 0)),
            pl.BlockSpec((ncmp, LANE), lambda b, g, qi, s: (0, 0)),
            pl.BlockSpec((LANE, T), lambda b, g, qi, s: (0, 0)),
            pl.BlockSpec((1, NSA_D), lambda b, g, qi, s: (0, 0)),
        ],
        out_specs=pl.BlockSpec((1, TQ, NSA_R * NSA_D), lambda b, g, qi, s: (b, qi, g)),
        scratch_shapes=[pltpu.VMEM((T, 2 * NSA_D), BF16)] * 4 + [pltpu.VMEM((T // KC + 1, TQ, KC), F32)]
        + [pltpu.VMEM((NSA_R * TQ, KC), F32)] * 2 + [pltpu.VMEM((NSA_R * TQ, KC), BF16)] * 2,
    )
    return pl.pallas_call(
        _nsa_kernel,
        grid_spec=grid_spec,
        out_shape=jax.ShapeDtypeStruct((B, T, NSA_WIDTH), BF16),
        compiler_params=_cparams(("parallel", "parallel", "arbitrary")),
        name="nsa_attention",
    )(slopes, proj3, proj3, proj3, proj3, proj3, proj3, kvc, ov, ex, ng)


def _outproj_kernel(hg_ref, ns_ref, x_ref, wo_ref, g2_ref, wr_ref, br_ref,
                    h1_ref, xs_ref, idx_ref, w_ref, xn_ref):
    h1_ref[...] = (x_ref[...]
                   + jnp.dot(hg_ref[...], wo_ref[0:HG_WIDTH, :], preferred_element_type=F32)
                   + jnp.dot(ns_ref[...], wo_ref[HG_WIDTH:, :], preferred_element_type=F32))

    def body(r, c):
        r0 = pl.multiple_of(r * NORM_ROWS, NORM_ROWS)
        xn = _rms_rows(h1_ref[pl.ds(r0, NORM_ROWS), :], g2_ref[...])
        xn_ref[pl.ds(r0, NORM_ROWS), :] = xn
        for s in range(SEG):
            xs_ref[pl.ds(r0 * SEG + s, NORM_ROWS, stride=SEG), :] = xn[:, s * LANE:(s + 1) * LANE]
        return c
    lax.fori_loop(0, TM_OUT // NORM_ROWS, body, 0)

    logits = jnp.dot(xn_ref[...], wr_ref[...], preferred_element_type=F32) + br_ref[...]
    lane = lax.broadcasted_iota(jnp.int32, logits.shape, 1)
    lg = jnp.where(lane < N_EXPERTS, logits, -jnp.inf)
    vals, idxs = [], []
    for _ in range(TOP_K):
        mx = jnp.max(lg, axis=-1, keepdims=True)
        ix = jnp.min(jnp.where(lg == mx, lane, LANE), axis=-1, keepdims=True)
        vals.append(mx)
        idxs.append(ix)
        lg = jnp.where(lane == ix, -jnp.inf, lg)
    es = [jnp.exp(v - vals[0]) for v in vals]
    den = es[0]
    for e in es[1:]:
        den = den + e
    idx_out = jnp.zeros(logits.shape, jnp.int32)
    w_out = jnp.zeros(logits.shape, F32)
    for k in range(TOP_K):
        idx_out = jnp.where(lane == k, idxs[k], idx_out)
        w_out = jnp.where(lane == k, es[k] / den, w_out)
    idx_ref[...] = idx_out
    w_ref[...] = w_out


def _outproj(y_hg, y_nsa, x2, wo, g2, wr_pad, br_pad):
    n_tok = x2.shape[0]
    row = lambda w: pl.BlockSpec((TM_OUT, w), lambda i: (i, 0))
    full = lambda a, b: pl.BlockSpec((a, b), lambda i: (0, 0))
    return pl.pallas_call(
        _outproj_kernel,
        grid=(n_tok // TM_OUT,),
        in_specs=[row(HG_WIDTH), row(NSA_WIDTH), row(D_MODEL), full(D_MODEL, D_MODEL), full(1, D_MODEL),
                  full(D_MODEL, LANE), full(1, LANE)],
        out_specs=[row(D_MODEL), pl.BlockSpec((TM_OUT * SEG, LANE), lambda i: (i, 0)), row(LANE), row(LANE)],
        out_shape=[jax.ShapeDtypeStruct((n_tok, D_MODEL), F32), jax.ShapeDtypeStruct((n_tok * SEG, LANE), F32),
                   jax.ShapeDtypeStruct((n_tok, LANE), jnp.int32), jax.ShapeDtypeStruct((n_tok, LANE), F32)],
        scratch_shapes=[pltpu.VMEM((TM_OUT, D_MODEL), F32)],
        compiler_params=_cparams(("parallel",)),
        name="outproj_router",
    )(y_hg, y_nsa, x2, wo, g2, wr_pad, br_pad)


def _moe_kernel(e_ref, sbase_ref, rows_ref, nu_ref, stok_ref,
                xs_hbm, wg_ref, wu_ref, wd_ref, bg_ref, bu_ref, bd_ref,
                y_ref, xstage, xg, act, sem):
    i = pl.program_id(0)
    s = pl.program_id(1)
    nu = nu_ref[0]
    n_asg = stok_ref.shape[0]
    nq = (rows_ref[i] + RQ_MOE - 1) // RQ_MOE
    unroll = 8

    def row_copy(tok, r):
        return pltpu.make_async_copy(xs_hbm.at[pl.ds(tok * SEG, SEG)], xstage.at[pl.ds(r * SEG, SEG)], sem.at[0])

    def issue(blk):
        base = sbase_ref[blk]
        nrow = (rows_ref[blk] + RQ_MOE - 1) // RQ_MOE * RQ_MOE

        def body(r8, c):
            for u in range(unroll):
                r = r8 * unroll + u
                row_copy(stok_ref[jnp.minimum(base + r, n_asg - 1)], r).start()
            return c
        lax.fori_loop(0, nrow // unroll, body, 0)

    @pl.when(s == 0)
    def _():
        @pl.when(i == 0)
        def _():
            issue(0)

        @pl.when(i < nu)
        def _():
            nrow = nq * RQ_MOE
            pltpu.make_async_copy(xs_hbm.at[pl.ds(0, nrow * SEG)], xstage.at[pl.ds(0, nrow * SEG)], sem.at[0]).wait()

    @pl.when((s == 1) & (i + 1 < nu))
    def _():
        issue(i + 1)

    def for_sub_blocks(fn):
        per = TS_MOE // RQ_MOE
        n_full = nq // per
        rem = nq % per

        def body(j, c):
            fn(pl.multiple_of(j * TS_MOE, TS_MOE), TS_MOE)
            return c
        lax.fori_loop(0, n_full, body, 0)
        r0 = n_full * TS_MOE
        size = TS_MOE // 2
        while size >= RQ_MOE:
            q = size // RQ_MOE

            @pl.when((rem & q) != 0)
            def _(r0=r0, size=size):
                fn(pl.multiple_of(r0, RQ_MOE), size)
            r0 = r0 + (rem & q) * RQ_MOE
            size //= 2

    @pl.when((s < NF_MOE) & (i < nu))
    def _():
        bg = bg_ref[0]
        bu = bu_ref[0]

        def phase_a(r0, size, from_stage):
            rs = pl.ds(r0, size)
            if from_stage:
                x = jnp.concatenate([xstage[pl.ds(r0 * SEG + sg, size, stride=SEG), :].astype(BF16)
                                     for sg in range(SEG)], axis=1)
                xg[rs, :] = x
            else:
                x = xg[rs, :]
            gate = jnp.dot(x, wg_ref[0].astype(BF16), preferred_element_type=F32) + bg
            up = jnp.dot(x, wu_ref[0].astype(BF16), preferred_element_type=F32) + bu
            gate = jnp.minimum(gate, SWIGLU_LIMIT)
            up = jnp.clip(up, -SWIGLU_LIMIT, SWIGLU_LIMIT)
            a = gate * jax.nn.sigmoid(SWIGLU_ALPHA * gate) * (up + 1.0)
            act[s, rs, :] = a.astype(BF16)

        @pl.when(s == 0)
        def _():
            for_sub_blocks(functools.partial(phase_a, from_stage=True))

        @pl.when(s > 0)
        def _():
            for_sub_blocks(functools.partial(phase_a, from_stage=False))

    @pl.when((s >= NF_MOE) & (i < nu))
    def _():
        bd = bd_ref[0]

        def phase_b(r0, size):
            rs = pl.ds(r0, size)
            a = jnp.concatenate([act[f, rs, :] for f in range(NF_MOE)], axis=1)
            y_ref[rs, :] = jnp.dot(a, wd_ref[0].astype(BF16), preferred_element_type=F32) + bd
        for_sub_blocks(phase_b)

        def zfill(j, c):
            y_ref[pl.ds(pl.multiple_of(j * RQ_MOE, RQ_MOE), RQ_MOE), :] = jnp.zeros((RQ_MOE, FCB_MOE), F32)
            return c
        lax.fori_loop(nq, SB_MOE // RQ_MOE, zfill, 0)

    @pl.when((s >= NF_MOE) & (i >= nu))
    def _():
        y_ref[...] = jnp.zeros_like(y_ref)


def _moe(sup_e, sup_base, sup_rows, n_used, stok, xs, wgu, wd, bgu, bd):
    ns = sup_e.shape[0]

    def e_of(i, e, nu):
        return e[jnp.where(i < nu[0], i, nu[0] - 1)]

    def f_of(i, s, nu):
        return jnp.where(i < nu[0], jnp.minimum(s, NF_MOE - 1), NF_MOE - 1)

    def n_of(i, s, nu):
        return jnp.where(i < nu[0], jnp.maximum(s - NF_MOE, 0), NN_MOE - 1)

    def wg_map(i, s, e, sb, rw, nu, tk):
        return (e_of(i, e, nu), 0, f_of(i, s, nu))

    def wu_map(i, s, e, sb, rw, nu, tk):
        return (e_of(i, e, nu), 0, NF_MOE + f_of(i, s, nu))

    def wd_map(i, s, e, sb, rw, nu, tk):
        return (e_of(i, e, nu), 0, n_of(i, s, nu))

    def y_map(i, s, e, sb, rw, nu, tk):
        return (i, jnp.maximum(s - NF_MOE, 0))

    grid_spec = pltpu.PrefetchScalarGridSpec(
        num_scalar_prefetch=5,
        grid=(ns, NF_MOE + NN_MOE),
        in_specs=[
            pl.BlockSpec(memory_space=pl.ANY),
            pl.BlockSpec((1, D_MODEL, FC_MOE), wg_map),
            pl.BlockSpec((1, D_MODEL, FC_MOE), wu_map),
            pl.BlockSpec((1, D_FF, FCB_MOE), wd_map),
            pl.BlockSpec((1, 1, FC_MOE), wg_map),
            pl.BlockSpec((1, 1, FC_MOE), wu_map),
            pl.BlockSpec((1, 1, FCB_MOE), wd_map),
        ],
        out_specs=pl.BlockSpec((SB_MOE, FCB_MOE), y_map),
        scratch_shapes=[pltpu.VMEM((SB_MOE * SEG, LANE), F32), pltpu.VMEM((SB_MOE, D_MODEL), BF16),
                        pltpu.VMEM((NF_MOE, SB_MOE, FC_MOE), BF16), pltpu.SemaphoreType.DMA((1,))],
    )
    return pl.pallas_call(
        _moe_kernel,
        grid_spec=grid_spec,
        out_shape=jax.ShapeDtypeStruct((ns * SB_MOE, D_MODEL), F32),
        compiler_params=_cparams(("arbitrary", "arbitrary")),
        name="moe_experts",
    )(sup_e, sup_base, sup_rows, n_used, stok, xs, wgu, wgu, wd, bgu, bgu, bd)


def _combine_kernel(pos_ref, y_hbm, h1_ref, w_ref, g_ref, o_ref, ybuf, sem):
    i = pl.program_id(0)
    n = pl.num_programs(0)
    slot = i % 2

    unroll = 4

    def row_copy(src, sl, k, r):
        return pltpu.make_async_copy(y_hbm.at[src], ybuf.at[sl, k * TC_CMB + r], sem.at[sl])

    def issue(blk, sl):
        def body(r4, c):
            for u in range(unroll):
                r = r4 * unroll + u
                for k in range(TOP_K):
                    row_copy(pos_ref[(blk * TC_CMB + r) * TOP_K + k], sl, k, r).start()
            return c
        lax.fori_loop(0, TC_CMB // unroll, body, 0)

    @pl.when(i == 0)
    def _():
        issue(0, 0)

    pltpu.make_async_copy(y_hbm.at[pl.ds(0, TOP_K * TC_CMB)], ybuf.at[slot], sem.at[slot]).wait()

    @pl.when(i + 1 < n)
    def _():
        issue(i + 1, 1 - slot)

    w = w_ref[...]
    h = h1_ref[...]
    for k in range(TOP_K):
        h = h + w[:, k:k + 1] * ybuf[slot, k * TC_CMB:(k + 1) * TC_CMB, :]
    o_ref[...] = _rms_rows(h, g_ref[...])


def _combine(pos, y_sorted, h1, top_w, g):
    n_tok = h1.shape[0]
    grid_spec = pltpu.PrefetchScalarGridSpec(
        num_scalar_prefetch=1,
        grid=(n_tok // TC_CMB,),
        in_specs=[
            pl.BlockSpec(memory_space=pl.ANY),
            pl.BlockSpec((TC_CMB, D_MODEL), lambda i, p: (i, 0)),
            pl.BlockSpec((TC_CMB, LANE), lambda i, p: (i, 0)),
            pl.BlockSpec((1, D_MODEL), lambda i, p: (0, 0)),
        ],
        out_specs=pl.BlockSpec((TC_CMB, D_MODEL), lambda i, p: (i, 0)),
        scratch_shapes=[pltpu.VMEM((2, TOP_K * TC_CMB, D_MODEL), F32), pltpu.SemaphoreType.DMA((2,))],
    )
    return pl.pallas_call(
        _combine_kernel,
        grid_spec=grid_spec,
        out_shape=jax.ShapeDtypeStruct((n_tok, D_MODEL), F32),
        compiler_params=_cparams(("arbitrary",)),
        name="moe_combine",
    )(pos, y_sorted, h1, top_w, g)


def _dispatch(top_idx):
    n_tok = top_idx.shape[0]
    n_asg = n_tok * TOP_K
    i32 = jnp.int32
    flat_e = top_idx.reshape(n_asg)
    order = jnp.argsort(flat_e).astype(i32)
    inv = jnp.argsort(order).astype(i32)
    stok = order // TOP_K
    eids = jnp.arange(N_EXPERTS, dtype=i32)
    counts = jnp.sum((flat_e[:, None] == eids[None, :]).astype(i32), axis=0)
    start = jnp.cumsum(counts) - counts
    nsup0 = (counts + SB_MOE - 1) // SB_MOE
    per = jnp.where(nsup0 > 0, (counts + nsup0 * TS_MOE - 1) // jnp.maximum(nsup0 * TS_MOE, 1) * TS_MOE, 0)
    nsup = jnp.where(per > 0, (counts + per - 1) // jnp.maximum(per, 1), 0)
    sup_end = jnp.cumsum(nsup)
    sup_start = sup_end - nsup
    ns = -(-n_asg // SB_MOE) + N_EXPERTS
    ii = jnp.arange(ns, dtype=i32)
    sup_e = jnp.minimum(jnp.sum((sup_end[None, :] <= ii[:, None]).astype(i32), axis=1), N_EXPERTS - 1)
    k_i = ii - sup_start[sup_e]
    sup_base = (start[sup_e] + k_i * per[sup_e]).astype(i32)
    sup_rows = jnp.clip(counts[sup_e] - k_i * per[sup_e], 0, per[sup_e]).astype(i32)
    n_used = sup_end[-1:].astype(i32)
    rank = inv - start[flat_e]
    pe = jnp.maximum(per[flat_e], 1)
    pos = ((sup_start[flat_e] + rank // pe) * SB_MOE + rank % pe).astype(i32)
    return sup_e.astype(i32), sup_base, sup_rows, n_used, stok, pos


def _alibi_slopes():
    h = np.arange(1, NSA_HEADS + 1, dtype=np.float32)
    return np.power(np.float32(2.0), -8.0 * h / NSA_HEADS).astype(np.float32).reshape(NSA_G, NSA_R)


def kernel(x, norm1_g, w_in, hg_lb_logits, hg_norm_g, cmp_pos, cmp_w1, cmp_w2, nsa_norm_g, w_out,
           norm2_g, w_router, b_router, w_gate_up, b_gate_up, w_down, b_down, final_norm_g):
    B, T, D = x.shape
    n_tok = B * T
    l = 0
    x2 = x.reshape(n_tok, D)

    w_in_pad = jnp.pad(w_in[l], ((0, 0), (0, IN_COLS_PAD - IN_COLS))).astype(BF16)
    lb_all = jnp.cumsum(jax.nn.softmax(hg_lb_logits.astype(F32), axis=0), axis=0)
    lb = lb_all[l].reshape(1, HG_WIDTH)
    w1r = cmp_w1[l].reshape(2, CMP_BLOCK, NSA_D, NSA_D).astype(BF16)
    w2b = cmp_w2[l].astype(BF16)
    wo = w_out[l].astype(BF16)
    wr_pad = jnp.pad(w_router[l], ((0, 0), (0, LANE - N_EXPERTS)))
    br_pad = jnp.pad(b_router[l], (0, LANE - N_EXPERTS)).reshape(1, LANE)
    bgu = b_gate_up[l].reshape(N_EXPERTS, 1, 2 * D_FF)
    bd = b_down[l].reshape(N_EXPERTS, 1, D_MODEL)

    proj = _inproj(x2, norm1_g[l].reshape(1, D), w_in_pad)
    proj3 = proj.reshape(B, T, IN_COLS_PAD)
    y_hg = _hgrn(proj3, lb, hg_norm_g[l].reshape(1, HG_DK))
    kvc = _compress(proj3, cmp_pos[l], w1r, w2b)
    y_nsa = _nsa(proj3, kvc, jnp.asarray(_alibi_slopes()), jnp.asarray(_overlap_matrix(T), BF16),
                 jnp.asarray(_expand_matrix(T), BF16), nsa_norm_g[l].reshape(1, NSA_D))
    h1, xs, top_idx, top_w = _outproj(y_hg.reshape(n_tok, HG_WIDTH), y_nsa.reshape(n_tok, NSA_WIDTH), x2, wo,
                                      norm2_g[l].reshape(1, D), wr_pad, br_pad)
    sup_e, sup_base, sup_rows, n_used, stok, pos = _dispatch(top_idx[:, :TOP_K])
    y_sorted = _moe(sup_e, sup_base, sup_rows, n_used, stok, xs, w_gate_up[l], w_down[l], bgu, bd)
    out = _combine(pos, y_sorted, h1, top_w, final_norm_g.reshape(1, D))
    return out.reshape(B, T, D)
```

```python
import functools

import numpy as np
import jax
import jax.numpy as jnp
from jax import lax
from jax.experimental import pallas as pl
from jax.experimental.pallas import tpu as pltpu

F32 = jnp.float32
BF16 = jnp.bfloat16

D_MODEL = 2048
HG_WIDTH = 1024
HG_DK = 128
HG_HEADS = 8
HG_CHUNK = 64
NSA_WIDTH = 1024
NSA_D = 128
NSA_HEADS = 8
NSA_G = 2
NSA_R = 4
KV_WIDTH = NSA_G * NSA_D
CMP_BLOCK = 32
CMP_STRIDE = 16
SEL_BLOCK = 64
SEL_TOP = 16
WINDOW = 512
N_EXPERTS = 32
TOP_K = 4
D_FF = 2048
SWIGLU_LIMIT = 7.0
SWIGLU_ALPHA = 1.702
EPS = 1e-6
NEG = -1e30
FORCED_SCORE = 1e4
IN_COLS = 4 * HG_WIDTH + NSA_WIDTH + 6 * KV_WIDTH + 3 * NSA_HEADS

LANE = 128
CB_HQ, CB_HF, CB_HI, CB_HG = 0, 8, 16, 24
CB_NQ = 32
CB_KCM, CB_VCM, CB_KSL, CB_VSL, CB_KWN, CB_VWN = 40, 42, 44, 46, 48, 50
CB_GATE = 52

TM_IN = 1024
TN_IN = 768
IN_COLS_PAD = 9 * TN_IN
NORM_ROWS = 32
HG_TT = 256
TQ = 256
KC = 512
NSA_SPLIT = 2
WSPAN = WINDOW + TQ
TM_OUT = 512
SEG = D_MODEL // LANE
SB_MOE = 1536
PQ_MOE = 512
TS_MOE = 1024
RQ_MOE = 128
FC_MOE = 256
FCB_MOE = 512
NF_MOE = D_FF // FC_MOE
NN_MOE = D_MODEL // FCB_MOE
TC_CMB = 128
VMEM_LIMIT = 56 * 1024 * 1024


def _cparams(sem):
    return pltpu.CompilerParams(dimension_semantics=sem, vmem_limit_bytes=VMEM_LIMIT)


def _rms_rows(x, g):
    ms = jnp.mean(x * x, axis=-1, keepdims=True)
    return x * lax.rsqrt(ms + EPS) * g


def _inproj_kernel(x_ref, g_ref, w_ref, o_ref, u_ref):
    @pl.when(pl.program_id(1) == 0)
    def _():
        def body(r, c):
            sl = pl.ds(pl.multiple_of(r * NORM_ROWS, NORM_ROWS), NORM_ROWS)
            u_ref[sl, :] = _rms_rows(x_ref[sl, :], g_ref[...]).astype(BF16)
            return c
        lax.fori_loop(0, TM_IN // NORM_ROWS, body, 0)

    o_ref[...] = jnp.dot(u_ref[...], w_ref[...], preferred_element_type=F32)


def _inproj(x2, g, w_pad):
    n_tok = x2.shape[0]
    return pl.pallas_call(
        _inproj_kernel,
        grid=(n_tok // TM_IN, IN_COLS_PAD // TN_IN),
        in_specs=[
            pl.BlockSpec((TM_IN, D_MODEL), lambda i, j: (i, 0)),
            pl.BlockSpec((1, D_MODEL), lambda i, j: (0, 0)),
            pl.BlockSpec((D_MODEL, TN_IN), lambda i, j: (0, j)),
        ],
        out_specs=pl.BlockSpec((TM_IN, TN_IN), lambda i, j: (i, j)),
        out_shape=jax.ShapeDtypeStruct((n_tok, IN_COLS_PAD), F32),
        scratch_shapes=[pltpu.VMEM((TM_IN, D_MODEL), BF16)],
        compiler_params=_cparams(("parallel", "arbitrary")),
        name="inproj",
    )(x2, g, w_pad)


def _hgrn_kernel(q_ref, f_ref, i_ref, g_ref, lb_ref, ng_ref, o_ref, st_ref, qd_ref, kd_ref, kr_ref, el_ref):
    C = HG_CHUNK
    nch = HG_TT // C
    ng = ng_ref[...]
    row = lax.broadcasted_iota(jnp.int32, (C, C), 0)
    col = lax.broadcasted_iota(jnp.int32, (C, C), 1)
    tril = row >= col
    tri_bf = jnp.where(tril, 1.0, 0.0).astype(BF16)
    lb = lb_ref[...]

    @pl.when(pl.program_id(1) == 0)
    def _():
        st_ref[...] = jnp.zeros_like(st_ref)

    for c in range(nch):
        sl = slice(c * C, (c + 1) * C)
        f = lb + (1.0 - lb) * jax.nn.sigmoid(f_ref[0, sl, :])
        logf = jnp.log(f)
        k = 1.0 - f
        hi = logf.astype(BF16)
        lo = (logf - hi.astype(F32)).astype(BF16)
        b = jnp.dot(tri_bf, hi, preferred_element_type=F32) + jnp.dot(tri_bf, lo, preferred_element_type=F32)
        e_last = jnp.exp(b[C - 1:C, :])
        k_dec = k * jnp.exp(-b)
        qd_ref[sl, :] = (jax.nn.silu(q_ref[0, sl, :]) * jnp.exp(b)).astype(BF16)
        kd_ref[sl, :] = k_dec.astype(BF16)
        kr_ref[sl, :] = (k_dec * e_last).astype(BF16)
        el_ref[c] = e_last

    for h in range(HG_HEADS):
        hs = slice(h * HG_DK, (h + 1) * HG_DK)
        st = st_ref[h]
        for c in range(nch):
            sl = slice(c * C, (c + 1) * C)
            q_dec = qd_ref[sl, hs]
            v_bf = i_ref[0, sl, hs].astype(BF16)
            a = lax.dot_general(q_dec, kd_ref[sl, hs], (((1,), (1,)), ((), ())), preferred_element_type=F32)
            a = jnp.where(tril, a, 0.0).astype(BF16)
            o = jnp.dot(a, v_bf, preferred_element_type=F32)
            o = o + lax.dot_general(q_dec, st.astype(BF16), (((1,), (1,)), ((), ())), preferred_element_type=F32)
            st = st * el_ref[c][:, hs] + lax.dot_general(v_bf, kr_ref[sl, hs], (((0,), (0,)), ((), ())),
                                                         preferred_element_type=F32)
            y = _rms_rows(o, ng) * jax.nn.silu(g_ref[0, sl, hs])
            o_ref[0, sl, hs] = y.astype(o_ref.dtype)
        st_ref[h] = st


def _hgrn(proj3, lb, ng):
    B, T, _ = proj3.shape

    def col(cb):
        return pl.BlockSpec((1, HG_TT, HG_WIDTH), lambda b, t, cb=cb: (b, t, cb // HG_HEADS))

    return pl.pallas_call(
        _hgrn_kernel,
        grid=(B, T // HG_TT),
        in_specs=[col(CB_HQ), col(CB_HF), col(CB_HI), col(CB_HG),
                  pl.BlockSpec((1, HG_WIDTH), lambda b, t: (0, 0)),
                  pl.BlockSpec((1, LANE), lambda b, t: (0, 0))],
        out_specs=pl.BlockSpec((1, HG_TT, HG_WIDTH), lambda b, t: (b, t, 0)),
        out_shape=jax.ShapeDtypeStruct((B, T, HG_WIDTH), BF16),
        scratch_shapes=[pltpu.VMEM((HG_HEADS, HG_DK, HG_DK), F32)] + [pltpu.VMEM((HG_TT, HG_WIDTH), BF16)] * 3
        + [pltpu.VMEM((HG_TT // HG_CHUNK, 1, HG_WIDTH), F32)],
        compiler_params=_cparams(("parallel", "arbitrary")),
        name="hgrn2",
    )(proj3, proj3, proj3, proj3, lb, ng)


def _gelu_tanh(x):
    return 0.5 * x * (1.0 + jnp.tanh(0.7978845608028654 * (x + 0.044715 * (x * x * x))))


def _compress_kernel(x_ref, pos_ref, w1_ref, w2_ref, o_ref):
    T = x_ref.shape[1]
    ng = T // CMP_STRIDE
    acc_a = jnp.zeros((ng, NSA_D), F32)
    acc_b = jnp.zeros((ng, NSA_D), F32)
    for r in range(CMP_STRIDE):
        xr = x_ref[0, pl.ds(r, ng, stride=CMP_STRIDE), :]
        xa = (xr + pos_ref[0, r:r + 1, :]).astype(BF16)
        xb = (xr + pos_ref[0, CMP_STRIDE + r:CMP_STRIDE + r + 1, :]).astype(BF16)
        acc_a = acc_a + jnp.dot(xa, w1_ref[0, r], preferred_element_type=F32)
        acc_b = acc_b + jnp.dot(xb, w1_ref[0, CMP_STRIDE + r], preferred_element_type=F32)
    hdn = _gelu_tanh(acc_a + pltpu.roll(acc_b, ng - 1, 0))
    o_ref[0, 0, 0] = jnp.dot(hdn.astype(BF16), w2_ref[0], preferred_element_type=F32)


def _compress(proj3, pos, w1r, w2b):
    B, T, _ = proj3.shape
    ng = T // CMP_STRIDE
    return pl.pallas_call(
        _compress_kernel,
        grid=(B, NSA_G, 2),
        in_specs=[
            pl.BlockSpec((1, T, LANE), lambda b, g, j: (b, 0, CB_KCM + 2 * j + g)),
            pl.BlockSpec((1, CMP_BLOCK, NSA_D), lambda b, g, j: (j, 0, 0)),
            pl.BlockSpec((1, CMP_BLOCK, NSA_D, NSA_D), lambda b, g, j: (j, 0, 0, 0)),
            pl.BlockSpec((1, NSA_D, NSA_D), lambda b, g, j: (j, 0, 0)),
        ],
        out_specs=pl.BlockSpec((1, 1, 1, ng, NSA_D), lambda b, g, j: (b, g, j, 0, 0)),
        out_shape=jax.ShapeDtypeStruct((B, NSA_G, 2, ng, NSA_D), F32),
        compiler_params=_cparams(("parallel", "parallel", "parallel")),
        name="nsa_compress",
    )(proj3, pos, w1r, w2b)


def _overlap_matrix(T):
    n_cmp = (T - CMP_BLOCK) // CMP_STRIDE + 1
    n_sel = T // SEL_BLOCK
    ratio_s = SEL_BLOCK // CMP_STRIDE
    ratio_c = CMP_BLOCK // CMP_STRIDE
    jj, mm, nn = np.meshgrid(np.arange(n_sel), np.arange(ratio_s), np.arange(ratio_c), indexing='ij')
    ii = ratio_s * jj + mm + nn - (ratio_c - 1)
    ok = (ii >= 0) & (ii < n_cmp)
    overlap = np.zeros((LANE, T // CMP_STRIDE), np.float32)
    np.add.at(overlap, (jj[ok], ii[ok]), 1.0)
    return overlap


MASK_OFF = -2.0 ** 100


def _expand_matrix(T):
    e = np.zeros((LANE, T), np.float32)
    e[np.arange(T) // SEL_BLOCK, np.arange(T)] = MASK_OFF
    return e


def _nsa_kernel(slopes_ref, q_ref, ks_ref, vs_ref, kw_ref, vw_ref, gt_ref, kvc_ref, ov_ref, e_ref, gq_ref, ng_ref,
                o_ref, ksb, vsb, kwb, vwb, bias_ref, s0_ref, s1_ref, p0_ref, p1_ref):
    T = ks_ref.shape[1]
    R, D = NSA_R, NSA_D
    N = R * TQ
    n_sel = T // SEL_BLOCK
    top_n = min(SEL_TOP, n_sel)
    g = pl.program_id(1)
    qi = pl.program_id(2)
    q0 = qi * TQ

    @pl.when(qi == 0)
    def _():
        cr = 256

        def body(r, c):
            r0 = pl.multiple_of(r * cr, cr)
            sl = pl.ds(r0, cr)
            kpos = r0 + lax.broadcasted_iota(jnp.int32, (cr, D), 0)
            lane_d = lax.broadcasted_iota(jnp.int32, (cr, D), 1)
            posf = jnp.where(lane_d == 0, kpos // SEL_BLOCK, jnp.where(lane_d == 1, kpos % SEL_BLOCK, 0))
            posf = posf.astype(F32).astype(BF16)
            ones = jnp.ones((cr, D), BF16)
            ksb[sl, 0:D] = ks_ref[0, sl, :].astype(BF16)
            ksb[sl, D:2 * D] = posf
            vsb[sl, 0:D] = vs_ref[0, sl, :].astype(BF16)
            vsb[sl, D:2 * D] = ones
            kwb[sl, 0:D] = kw_ref[0, sl, :].astype(BF16)
            kwb[sl, D:2 * D] = posf
            vwb[sl, 0:D] = vw_ref[0, sl, :].astype(BF16)
            vwb[sl, D:2 * D] = ones
            return c
        lax.fori_loop(0, T // cr, body, 0)
        bias_ref[T // KC] = jnp.full((TQ, KC), NEG, F32)

    def stack(x):
        return jnp.concatenate([x] * R, axis=0)

    q = q_ref[0] * (D ** -0.5)
    qs = jnp.concatenate([q[:, r * D:(r + 1) * D] for r in range(R)], axis=0).astype(BF16)
    slope = jnp.concatenate([jnp.full((TQ, 1), slopes_ref[g, r], F32) for r in range(R)], axis=0)
    lane_n = lax.broadcasted_iota(jnp.int32, (N, D), 1)
    slopef = jnp.where(lane_n == 0, slope * float(SEL_BLOCK), jnp.where(lane_n == 1, slope, 0.0))
    q_aug = jnp.concatenate([qs, slopef.astype(BF16)], axis=1)
    t1 = q0 + lax.broadcasted_iota(jnp.int32, (TQ, 1), 0)
    trow = stack(t1)

    kc = kvc_ref[0, 0, 0].astype(BF16)
    vc = kvc_ref[0, 0, 1].astype(BF16)
    ncmp = kc.shape[0]
    cend = lax.broadcasted_iota(jnp.int32, (1, ncmp), 1) * CMP_STRIDE + (CMP_BLOCK - 1)
    cend = jnp.where(cend < T, cend, 4 * T)
    valid_c = trow >= cend
    s = lax.dot_general(qs, kc, (((1,), (1,)), ((), ())), preferred_element_type=F32)
    s = s + slope * cend.astype(F32)
    s = jnp.where(valid_c, s, NEG)
    mx = jnp.max(s, axis=-1, keepdims=True)
    e = jnp.where(valid_c, jnp.exp(s - mx), 0.0)
    l = jnp.sum(e, axis=-1, keepdims=True)
    p_cmp = e / jnp.where(l > 0.0, l, 1.0)
    o_cmp = jnp.dot(p_cmp.astype(BF16), vc, preferred_element_type=F32)

    cend_t = lax.broadcasted_iota(jnp.int32, (ncmp, N), 0) * CMP_STRIDE + (CMP_BLOCK - 1)
    cend_t = jnp.where(cend_t < T, cend_t, 4 * T)
    t_t = q0 + (lax.broadcasted_iota(jnp.int32, (ncmp, N), 1) & (TQ - 1))
    valid_t = t_t >= cend_t
    slope_row = jnp.concatenate([jnp.full((1, TQ), slopes_ref[g, r], F32) for r in range(R)], axis=1)
    st = lax.dot_general(kc, qs, (((1,), (1,)), ((), ())), preferred_element_type=F32)
    st = jnp.where(valid_t, st + slope_row * cend_t.astype(F32), NEG)
    et = jnp.where(valid_t, jnp.exp(st - jnp.max(st, axis=0, keepdims=True)), 0.0)
    lt = jnp.sum(et, axis=0, keepdims=True)
    pt = et / jnp.where(lt > 0.0, lt, 1.0)
    psum = pt[:, 0:TQ]
    for r in range(1, R):
        psum = psum + pt[:, r * TQ:(r + 1) * TQ]
    ph = psum.astype(BF16)
    plo = (psum - ph.astype(F32)).astype(BF16)
    ovt = ov_ref[...]
    imp = (jnp.dot(ovt, ph, preferred_element_type=F32) + jnp.dot(ovt, plo, preferred_element_type=F32))[0:n_sel]

    blk = lax.broadcasted_iota(jnp.int32, (n_sel, TQ), 0)
    cur = (q0 + lax.broadcasted_iota(jnp.int32, (n_sel, TQ), 1)) // SEL_BLOCK
    forced = (blk == 0) | (blk == cur) | (blk == cur - 1)
    score = jnp.where(forced, FORCED_SCORE, jnp.where(blk <= cur, imp, -FORCED_SCORE))
    rank = jnp.zeros((n_sel, TQ), F32)
    for kk in range(n_sel):
        sk = score[kk:kk + 1, :]
        beats = jnp.where(sk > score, 1.0, jnp.where(sk == score, jnp.where(blk > kk, 1.0, 0.0), 0.0))
        rank = rank + beats
    notsel = jnp.where(rank < float(top_n), 0.0, 1.0).astype(BF16)
    nchunks = T // KC
    for c in range(nchunks):
        bias_ref[c] = lax.dot_general(notsel, e_ref[0:n_sel, c * KC:(c + 1) * KC], (((0,), (0,)), ((), ())),
                                      preferred_element_type=F32)
    cd = q0 // KC
    kpos = cd * KC + lax.broadcasted_iota(jnp.int32, (TQ, KC), 1)
    bias_ref[cd] = bias_ref[cd] + jnp.where(kpos <= t1, 0.0, NEG)

    start = pl.multiple_of(jnp.maximum(q0 - WINDOW, 0), TQ)
    dist = (t1 - start) - lax.broadcasted_iota(jnp.int32, (TQ, WSPAN), 1)
    bias_w = jnp.where(dist >= 0, jnp.where(dist < WINDOW, 0.0, NEG), NEG)
    kw = kwb[pl.ds(start, WSPAN), :]
    vw = vwb[pl.ds(start, WSPAN), :]
    o_win_parts = []
    for hf in range(NSA_SPLIT):
        rows = slice(hf * N // NSA_SPLIT, (hf + 1) * N // NSA_SPLIT)
        sw = lax.dot_general(q_aug[rows], kw, (((1,), (1,)), ((), ())), preferred_element_type=F32)
        sw = jnp.concatenate([sw[r * TQ:(r + 1) * TQ] + bias_w for r in range(R // NSA_SPLIT)], axis=0)
        pw = jnp.exp(sw - jnp.max(sw, axis=-1, keepdims=True))
        acc_w = jnp.dot(pw.astype(BF16), vw, preferred_element_type=F32)
        o_win_parts.append(acc_w[:, 0:D] / acc_w[:, D:D + 1])
    o_win = jnp.concatenate(o_win_parts, axis=0)

    n_ch = cd + 1

    def scores(c):
        k0 = pl.multiple_of(jnp.minimum(c, nchunks - 1) * KC, KC)
        return lax.dot_general(q_aug, ksb[pl.ds(k0, KC), :], (((1,), (1,)), ((), ())), preferred_element_type=F32)

    def soft(s_ref, c, m):
        bias = bias_ref[jnp.where(c < n_ch, c, nchunks)]
        sm = jnp.concatenate([s_ref[r * TQ:(r + 1) * TQ, :] + bias for r in range(R)], axis=0)
        m_new = jnp.maximum(m, jnp.max(sm, axis=-1, keepdims=True))
        return m_new, jnp.exp(m - m_new), jnp.exp(sm - m_new).astype(BF16)

    def pv(p_ref, c, alpha, acc):
        k0 = pl.multiple_of(jnp.minimum(c, nchunks - 1) * KC, KC)
        return alpha * acc + jnp.dot(p_ref[...], vsb[pl.ds(k0, KC), :], preferred_element_type=F32)

    s0_ref[...] = scores(0)
    m1, alpha1, p = soft(s0_ref, 0, jnp.full((N, 1), NEG, F32))
    p0_ref[...] = p
    s1_ref[...] = scores(1)

    def sel_body(jj, carry):
        m, alpha, acc = carry
        j = 2 * jj
        acc = pv(p0_ref, j, alpha, acc)
        m, alpha, p = soft(s1_ref, j + 1, m)
        p1_ref[...] = p
        s0_ref[...] = scores(j + 2)
        acc = pv(p1_ref, j + 1, alpha, acc)
        m, alpha, p = soft(s0_ref, j + 2, m)
        p0_ref[...] = p
        s1_ref[...] = scores(j + 3)
        return m, alpha, acc

    _, _, acc_sel = lax.fori_loop(0, (n_ch + 1) // 2, sel_body, (m1, alpha1, jnp.zeros((N, 2 * D), F32)))
    o_sel = acc_sel[:, 0:D] / acc_sel[:, D:D + 1]

    sig = jax.nn.sigmoid(gt_ref[0])
    sig_h = sig.astype(BF16)
    sig_l = (sig - sig_h.astype(F32)).astype(BF16)
    gq = gq_ref[0]
    gates = jnp.dot(sig_h, gq, preferred_element_type=F32) + jnp.dot(sig_l, gq, preferred_element_type=F32)
    parts = []
    for r in range(R):
        rows = slice(r * TQ, (r + 1) * TQ)
        g3 = [gates[:, (3 * r + br) * D:(3 * r + br + 1) * D] for br in range(3)]
        parts.append(g3[0] * o_cmp[rows] + g3[1] * o_sel[rows] + g3[2] * o_win[rows])
    o = jnp.concatenate(parts, axis=0)
    y = _rms_rows(o, ng_ref[...])
    for r in range(R):
        o_ref[0, :, r * D:(r + 1) * D] = y[r * TQ:(r + 1) * TQ].astype(o_ref.dtype)


def _gate_copy_matrix():
    m = np.zeros((NSA_G, LANE, 3 * NSA_R * NSA_D), np.float32)
    for g in range(NSA_G):
        for r in range(NSA_R):
            for br in range(3):
                m[g, (g * NSA_R + r) * 3 + br, (3 * r + br) * NSA_D:(3 * r + br + 1) * NSA_D] = 1.0
    return m


def _nsa(proj3, kvc, ng):
    B, T, _ = proj3.shape
    ncmp = T // CMP_STRIDE
    slopes = jnp.asarray(_alibi_slopes())
    ov = jnp.asarray(_overlap_matrix(T), BF16)
    ex = jnp.asarray(_expand_matrix(T), BF16)
    gq = jnp.asarray(_gate_copy_matrix(), BF16)

    def kv(cb):
        return pl.BlockSpec((1, T, LANE), lambda b, g, qi, s, cb=cb: (b, 0, cb + g))

    grid_spec = pltpu.PrefetchScalarGridSpec(
        num_scalar_prefetch=1,
        grid=(B, NSA_G, T // TQ),
        in_specs=[
            pl.BlockSpec((1, TQ, NSA_R * NSA_D), lambda b, g, qi, s: (b, qi, CB_NQ // NSA_R + g)),
            kv(CB_KSL), kv(CB_VSL), kv(CB_KWN), kv(CB_VWN),
            pl.BlockSpec((1, TQ, LANE), lambda b, g, qi, s: (b, qi, CB_GATE)),
            pl.BlockSpec((1, 1, 2, ncmp, NSA_D), lambda b, g, qi, s: (b, g, 0, 0, 0)),
            pl.BlockSpec((LANE, ncmp), lambda b, g, qi, s: (0, 0)),
            pl.BlockSpec((LANE, T), lambda b, g, qi, s: (0, 0)),
            pl.BlockSpec((1, LANE, 3 * NSA_R * NSA_D), lambda b, g, qi, s: (g, 0, 0)),
            pl.BlockSpec((1, NSA_D), lambda b, g, qi, s: (0, 0)),
        ],
        out_specs=pl.BlockSpec((1, TQ, NSA_R * NSA_D), lambda b, g, qi, s: (b, qi, g)),
        scratch_shapes=[pltpu.VMEM((T, 2 * NSA_D), BF16)] * 4 + [pltpu.VMEM((T // KC + 1, TQ, KC), F32)]
        + [pltpu.VMEM((NSA_R * TQ, KC), F32)] * 2 + [pltpu.VMEM((NSA_R * TQ, KC), BF16)] * 2,
    )
    return pl.pallas_call(
        _nsa_kernel,
        grid_spec=grid_spec,
        out_shape=jax.ShapeDtypeStruct((B, T, NSA_WIDTH), BF16),
        compiler_params=_cparams(("parallel", "parallel", "arbitrary")),
        name="nsa_attention",
    )(slopes, proj3, proj3, proj3, proj3, proj3, proj3, kvc, ov, ex, gq, ng)


def _outproj_kernel(hg_ref, ns_ref, x_ref, wo_ref, g2_ref, wr_ref, br_ref,
                    h1_ref, xs_ref, idx_ref, w_ref, xh_ref, xl_ref):
    h1_ref[...] = (x_ref[...]
                   + jnp.dot(hg_ref[...], wo_ref[0:HG_WIDTH, :], preferred_element_type=F32)
                   + jnp.dot(ns_ref[...], wo_ref[HG_WIDTH:, :], preferred_element_type=F32))

    def body(r, c):
        r0 = pl.multiple_of(r * NORM_ROWS, NORM_ROWS)
        xn = _rms_rows(h1_ref[pl.ds(r0, NORM_ROWS), :], g2_ref[...])
        xh = xn.astype(BF16)
        xh_ref[pl.ds(r0, NORM_ROWS), :] = xh
        xl_ref[pl.ds(r0, NORM_ROWS), :] = (xn - xh.astype(F32)).astype(BF16)
        for s in range(SEG):
            xs_ref[pl.ds(r0 * SEG + s, NORM_ROWS, stride=SEG), :] = xn[:, s * LANE:(s + 1) * LANE]
        return c
    lax.fori_loop(0, TM_OUT // NORM_ROWS, body, 0)

    hh = jnp.dot(xh_ref[...], wr_ref[...], preferred_element_type=F32)
    lh = jnp.dot(xl_ref[...], wr_ref[:, 0:LANE], preferred_element_type=F32)
    logits = hh[:, 0:LANE] + hh[:, LANE:2 * LANE] + lh + br_ref[...]
    lane = lax.broadcasted_iota(jnp.int32, logits.shape, 1)
    lg = jnp.where(lane < N_EXPERTS, logits, -jnp.inf)
    vals, idxs = [], []
    for _ in range(TOP_K):
        mx = jnp.max(lg, axis=-1, keepdims=True)
        ix = jnp.min(jnp.where(lg == mx, lane, LANE), axis=-1, keepdims=True)
        vals.append(mx)
        idxs.append(ix)
        lg = jnp.where(lane == ix, -jnp.inf, lg)
    es = [jnp.exp(v - vals[0]) for v in vals]
    den = es[0]
    for e in es[1:]:
        den = den + e
    idx_out = jnp.zeros(logits.shape, jnp.int32)
    w_out = jnp.zeros(logits.shape, F32)
    for k in range(TOP_K):
        idx_out = jnp.where(lane == k, idxs[k], idx_out)
        w_out = jnp.where(lane == k, es[k] / den, w_out)
    idx_ref[...] = idx_out
    w_ref[...] = w_out


def _outproj(y_hg, y_nsa, x2, wo, g2, wr_pad, br_pad):
    n_tok = x2.shape[0]
    row = lambda w: pl.BlockSpec((TM_OUT, w), lambda i: (i, 0))
    full = lambda a, b: pl.BlockSpec((a, b), lambda i: (0, 0))
    return pl.pallas_call(
        _outproj_kernel,
        grid=(n_tok // TM_OUT,),
        in_specs=[row(HG_WIDTH), row(NSA_WIDTH), row(D_MODEL), full(D_MODEL, D_MODEL), full(1, D_MODEL),
                  full(D_MODEL, 2 * LANE), full(1, LANE)],
        out_specs=[row(D_MODEL), pl.BlockSpec((TM_OUT * SEG, LANE), lambda i: (i, 0)), row(LANE), row(LANE)],
        out_shape=[jax.ShapeDtypeStruct((n_tok, D_MODEL), F32), jax.ShapeDtypeStruct((n_tok * SEG, LANE), F32),
                   jax.ShapeDtypeStruct((n_tok, LANE), jnp.int32), jax.ShapeDtypeStruct((n_tok, LANE), F32)],
        scratch_shapes=[pltpu.VMEM((TM_OUT, D_MODEL), BF16)] * 2,
        compiler_params=_cparams(("parallel",)),
        name="outproj_router",
    )(y_hg, y_nsa, x2, wo, g2, wr_pad, br_pad)


def _moe_kernel(e_ref, sbase_ref, rows_ref, nu_ref, stok_ref,
                xs_hbm, wg_ref, wu_ref, wd_ref, bg_ref, bu_ref, bd_ref,
                y_ref, xstage, xg, act, sem):
    i = pl.program_id(0)
    s = pl.program_id(1)
    nu = nu_ref[0]
    n_asg = stok_ref.shape[0]
    nq = (rows_ref[i] + RQ_MOE - 1) // RQ_MOE
    unroll = 8

    def row_copy(tok, r):
        return pltpu.make_async_copy(xs_hbm.at[pl.ds(tok * SEG, SEG)], xstage.at[pl.ds(r * SEG, SEG)], sem.at[0])

    def issue(blk):
        base = sbase_ref[blk]
        nrow = (rows_ref[blk] + RQ_MOE - 1) // RQ_MOE * RQ_MOE

        def body(r8, c):
            for u in range(unroll):
                r = r8 * unroll + u
                row_copy(stok_ref[jnp.minimum(base + r, n_asg - 1)], r).start()
            return c
        lax.fori_loop(0, nrow // unroll, body, 0)

    @pl.when(s == 0)
    def _():
        @pl.when(i == 0)
        def _():
            issue(0)

        @pl.when(i < nu)
        def _():
            nrow = nq * RQ_MOE
            pltpu.make_async_copy(xs_hbm.at[pl.ds(0, nrow * SEG)], xstage.at[pl.ds(0, nrow * SEG)], sem.at[0]).wait()

    @pl.when((s == 1) & (i + 1 < nu))
    def _():
        issue(i + 1)

    def for_sub_blocks(fn):
        per = TS_MOE // RQ_MOE
        n_full = nq // per
        rem = nq % per

        def body(j, c):
            fn(pl.multiple_of(j * TS_MOE, TS_MOE), TS_MOE)
            return c
        lax.fori_loop(0, n_full, body, 0)
        r0 = n_full * TS_MOE
        size = TS_MOE // 2
        while size >= RQ_MOE:
            q = size // RQ_MOE

            @pl.when((rem & q) != 0)
            def _(r0=r0, size=size):
                fn(pl.multiple_of(r0, RQ_MOE), size)
            r0 = r0 + (rem & q) * RQ_MOE
            size //= 2

    @pl.when((s < NF_MOE) & (i < nu))
    def _():
        bg = bg_ref[0]
        bu = bu_ref[0]

        def phase_a(r0, size, from_stage):
            rs = pl.ds(r0, size)
            if from_stage:
                x = jnp.concatenate([xstage[pl.ds(r0 * SEG + sg, size, stride=SEG), :].astype(BF16)
                                     for sg in range(SEG)], axis=1)
                xg[rs, :] = x
            else:
                x = xg[rs, :]
            gate = jnp.dot(x, wg_ref[0].astype(BF16), preferred_element_type=F32) + bg
            up = jnp.dot(x, wu_ref[0].astype(BF16), preferred_element_type=F32) + bu
            gate = jnp.minimum(gate, SWIGLU_LIMIT)
            up = jnp.clip(up, -SWIGLU_LIMIT, SWIGLU_LIMIT)
            a = gate * jax.nn.sigmoid(SWIGLU_ALPHA * gate) * (up + 1.0)
            act[s, rs, :] = a.astype(BF16)

        @pl.when(s == 0)
        def _():
            for_sub_blocks(functools.partial(phase_a, from_stage=True))

        @pl.when(s > 0)
        def _():
            for_sub_blocks(functools.partial(phase_a, from_stage=False))

    @pl.when((s >= NF_MOE) & (i < nu))
    def _():
        bd = bd_ref[0]

        def phase_b(r0, size):
            rs = pl.ds(r0, size)
            a = jnp.concatenate([act[f, rs, :] for f in range(NF_MOE)], axis=1)
            y_ref[rs, :] = jnp.dot(a, wd_ref[0].astype(BF16), preferred_element_type=F32) + bd
        for_sub_blocks(phase_b)

        def zfill(j, c):
            y_ref[pl.ds(pl.multiple_of(j * RQ_MOE, RQ_MOE), RQ_MOE), :] = jnp.zeros((RQ_MOE, FCB_MOE), F32)
            return c
        lax.fori_loop(nq, SB_MOE // RQ_MOE, zfill, 0)

    @pl.when((s >= NF_MOE) & (i >= nu))
    def _():
        y_ref[...] = jnp.zeros_like(y_ref)


def _moe(sup_e, sup_base, sup_rows, n_used, stok, xs, wgu, wd, bgu, bd):
    ns = sup_e.shape[0]

    def e_of(i, e, nu):
        return e[jnp.where(i < nu[0], i, nu[0] - 1)]

    def f_of(i, s, nu):
        return jnp.where(i < nu[0], jnp.minimum(s, NF_MOE - 1), NF_MOE - 1)

    def n_of(i, s, nu):
        return jnp.where(i < nu[0], jnp.maximum(s - NF_MOE, 0), NN_MOE - 1)

    def wg_map(i, s, e, sb, rw, nu, tk):
        return (e_of(i, e, nu), 0, f_of(i, s, nu))

    def wu_map(i, s, e, sb, rw, nu, tk):
        return (e_of(i, e, nu), 0, NF_MOE + f_of(i, s, nu))

    def wd_map(i, s, e, sb, rw, nu, tk):
        return (e_of(i, e, nu), 0, n_of(i, s, nu))

    def y_map(i, s, e, sb, rw, nu, tk):
        return (i, jnp.maximum(s - NF_MOE, 0))

    grid_spec = pltpu.PrefetchScalarGridSpec(
        num_scalar_prefetch=5,
        grid=(ns, NF_MOE + NN_MOE),
        in_specs=[
            pl.BlockSpec(memory_space=pl.ANY),
            pl.BlockSpec((1, D_MODEL, FC_MOE), wg_map),
            pl.BlockSpec((1, D_MODEL, FC_MOE), wu_map),
            pl.BlockSpec((1, D_FF, FCB_MOE), wd_map),
            pl.BlockSpec((1, 1, FC_MOE), wg_map),
            pl.BlockSpec((1, 1, FC_MOE), wu_map),
            pl.BlockSpec((1, 1, FCB_MOE), wd_map),
        ],
        out_specs=pl.BlockSpec((SB_MOE, FCB_MOE), y_map),
        scratch_shapes=[pltpu.VMEM((SB_MOE * SEG, LANE), F32), pltpu.VMEM((SB_MOE, D_MODEL), BF16),
                        pltpu.VMEM((NF_MOE, SB_MOE, FC_MOE), BF16), pltpu.SemaphoreType.DMA((1,))],
    )
    return pl.pallas_call(
        _moe_kernel,
        grid_spec=grid_spec,
        out_shape=jax.ShapeDtypeStruct((ns * SB_MOE, D_MODEL), F32),
        compiler_params=_cparams(("arbitrary", "arbitrary")),
        name="moe_experts",
    )(sup_e, sup_base, sup_rows, n_used, stok, xs, wgu, wgu, wd, bgu, bgu, bd)


def _combine_kernel(pos_ref, y_hbm, h1_ref, w_ref, g_ref, o_ref, ybuf, sem):
    i = pl.program_id(0)
    n = pl.num_programs(0)
    slot = i % 2

    unroll = 4

    def row_copy(src, sl, k, r):
        return pltpu.make_async_copy(y_hbm.at[src], ybuf.at[sl, k * TC_CMB + r], sem.at[sl])

    def issue(blk, sl):
        def body(r4, c):
            for u in range(unroll):
                r = r4 * unroll + u
                for k in range(TOP_K):
                    row_copy(pos_ref[(blk * TC_CMB + r) * TOP_K + k], sl, k, r).start()
            return c
        lax.fori_loop(0, TC_CMB // unroll, body, 0)

    @pl.when(i == 0)
    def _():
        issue(0, 0)

    pltpu.make_async_copy(y_hbm.at[pl.ds(0, TOP_K * TC_CMB)], ybuf.at[slot], sem.at[slot]).wait()

    @pl.when(i + 1 < n)
    def _():
        issue(i + 1, 1 - slot)

    w = w_ref[...]
    h = h1_ref[...]
    for k in range(TOP_K):
        h = h + w[:, k:k + 1] * ybuf[slot, k * TC_CMB:(k + 1) * TC_CMB, :]
    o_ref[...] = _rms_rows(h, g_ref[...])


def _combine(pos, y_sorted, h1, top_w, g):
    n_tok = h1.shape[0]
    grid_spec = pltpu.PrefetchScalarGridSpec(
        num_scalar_prefetch=1,
        grid=(n_tok // TC_CMB,),
        in_specs=[
            pl.BlockSpec(memory_space=pl.ANY),
            pl.BlockSpec((TC_CMB, D_MODEL), lambda i, p: (i, 0)),
            pl.BlockSpec((TC_CMB, LANE), lambda i, p: (i, 0)),
            pl.BlockSpec((1, D_MODEL), lambda i, p: (0, 0)),
        ],
        out_specs=pl.BlockSpec((TC_CMB, D_MODEL), lambda i, p: (i, 0)),
        scratch_shapes=[pltpu.VMEM((2, TOP_K * TC_CMB, D_MODEL), F32), pltpu.SemaphoreType.DMA((2,))],
    )
    return pl.pallas_call(
        _combine_kernel,
        grid_spec=grid_spec,
        out_shape=jax.ShapeDtypeStruct((n_tok, D_MODEL), F32),
        compiler_params=_cparams(("arbitrary",)),
        name="moe_combine",
    )(pos, y_sorted, h1, top_w, g)


def _dispatch(top_idx):
    n_tok = top_idx.shape[0]
    n_asg = n_tok * TOP_K
    i32 = jnp.int32
    flat_e = top_idx.reshape(n_asg)
    order = jnp.argsort(flat_e).astype(i32)
    inv = jnp.argsort(order).astype(i32)
    stok = order // TOP_K
    eids = jnp.arange(N_EXPERTS, dtype=i32)
    counts = jnp.sum((flat_e[:, None] == eids[None, :]).astype(i32), axis=0)
    start = jnp.cumsum(counts) - counts
    nsup0 = (counts + SB_MOE - 1) // SB_MOE
    per = jnp.where(nsup0 > 0, (counts + nsup0 * PQ_MOE - 1) // jnp.maximum(nsup0 * PQ_MOE, 1) * PQ_MOE, 0)
    nsup = jnp.where(per > 0, (counts + per - 1) // jnp.maximum(per, 1), 0)
    sup_end = jnp.cumsum(nsup)
    sup_start = sup_end - nsup
    ns = -(-n_asg // SB_MOE) + N_EXPERTS
    ii = jnp.arange(ns, dtype=i32)
    sup_e = jnp.minimum(jnp.sum((sup_end[None, :] <= ii[:, None]).astype(i32), axis=1), N_EXPERTS - 1)
    k_i = ii - sup_start[sup_e]
    sup_base = (start[sup_e] + k_i * per[sup_e]).astype(i32)
    sup_rows = jnp.clip(counts[sup_e] - k_i * per[sup_e], 0, per[sup_e]).astype(i32)
    n_used = sup_end[-1:].astype(i32)
    rank = inv - start[flat_e]
    pe = jnp.maximum(per[flat_e], 1)
    pos = ((sup_start[flat_e] + rank // pe) * SB_MOE + rank % pe).astype(i32)
    return sup_e.astype(i32), sup_base, sup_rows, n_used, stok, pos


def _alibi_slopes():
    h = np.arange(1, NSA_HEADS + 1, dtype=np.float32)
    return np.power(np.float32(2.0), -8.0 * h / NSA_HEADS).astype(np.float32).reshape(NSA_G, NSA_R)


def kernel(x, norm1_g, w_in, hg_lb_logits, hg_norm_g, cmp_pos, cmp_w1, cmp_w2, nsa_norm_g, w_out,
           norm2_g, w_router, b_router, w_gate_up, b_gate_up, w_down, b_down, final_norm_g):
    B, T, D = x.shape
    n_tok = B * T
    l = 0
    x2 = x.reshape(n_tok, D)

    w_in_pad = jnp.pad(w_in[l], ((0, 0), (0, IN_COLS_PAD - IN_COLS))).astype(BF16)
    lb_all = jnp.cumsum(jax.nn.softmax(hg_lb_logits.astype(F32), axis=0), axis=0)
    lb = lb_all[l].reshape(1, HG_WIDTH)
    w1r = cmp_w1[l].reshape(2, CMP_BLOCK, NSA_D, NSA_D).astype(BF16)
    w2b = cmp_w2[l].astype(BF16)
    wo = w_out[l].astype(BF16)
    wr_f = jnp.pad(w_router[l], ((0, 0), (0, LANE - N_EXPERTS)))
    wr_h = wr_f.astype(BF16)
    wr_pad = jnp.concatenate([wr_h, (wr_f - wr_h.astype(F32)).astype(BF16)], axis=1)
    br_pad = jnp.pad(b_router[l], (0, LANE - N_EXPERTS)).reshape(1, LANE)
    bgu = b_gate_up[l].reshape(N_EXPERTS, 1, 2 * D_FF)
    bd = b_down[l].reshape(N_EXPERTS, 1, D_MODEL)

    proj = _inproj(x2, norm1_g[l].reshape(1, D), w_in_pad)
    proj3 = proj.reshape(B, T, IN_COLS_PAD)
    y_hg = _hgrn(proj3, lb, hg_norm_g[l].reshape(1, HG_DK))
    kvc = _compress(proj3, cmp_pos[l], w1r, w2b)
    y_nsa = _nsa(proj3, kvc, nsa_norm_g[l].reshape(1, NSA_D))
    h1, xs, top_idx, top_w = _outproj(y_hg.reshape(n_tok, HG_WIDTH), y_nsa.reshape(n_tok, NSA_WIDTH), x2, wo,
                                      norm2_g[l].reshape(1, D), wr_pad, br_pad)
    sup_e, sup_base, sup_rows, n_used, stok, pos = _dispatch(top_idx[:, :TOP_K])
    y_sorted = _moe(sup_e, sup_base, sup_rows, n_used, stok, xs, w_gate_up[l], w_down[l], bgu, bd)
    out = _combine(pos, y_sorted, h1, top_w, final_norm_g.reshape(1, D))
    return out.reshape(B, T, D)
```

```python
import functools

import numpy as np
import jax
import jax.numpy as jnp
from jax import lax
from jax.experimental import pallas as pl
from jax.experimental.pallas import tpu as pltpu

F32 = jnp.float32
BF16 = jnp.bfloat16

D_MODEL = 2048
HG_WIDTH = 1024
HG_DK = 128
HG_HEADS = 8
HG_CHUNK = 64
NSA_WIDTH = 1024
NSA_D = 128
NSA_HEADS = 8
NSA_G = 2
NSA_R = 4
KV_WIDTH = NSA_G * NSA_D
CMP_BLOCK = 32
CMP_STRIDE = 16
SEL_BLOCK = 64
SEL_TOP = 16
WINDOW = 512
N_EXPERTS = 32
TOP_K = 4
D_FF = 2048
SWIGLU_LIMIT = 7.0
SWIGLU_ALPHA = 1.702
EPS = 1e-6
NEG = -1e30
FORCED_SCORE = 1e4
IN_COLS = 4 * HG_WIDTH + NSA_WIDTH + 6 * KV_WIDTH + 3 * NSA_HEADS

LANE = 128
CB_HQ, CB_HF, CB_HI, CB_HG = 0, 8, 16, 24
CB_NQ = 32
CB_KCM, CB_VCM, CB_KSL, CB_VSL, CB_KWN, CB_VWN = 40, 42, 44, 46, 48, 50
CB_GATE = 52

TM_IN = 1024
TN_IN = 768
IN_COLS_PAD = 9 * TN_IN
NORM_ROWS = 32
HG_TT = 256
TQ = 256
KC = 512
NSA_SPLIT = 2
WSPAN = WINDOW + TQ
TM_OUT = 512
SEG = D_MODEL // LANE
SB_MOE = 1536
PQ_MOE = 512
TS_MOE = 1024
RQ_MOE = 128
FC_MOE = 256
FCB_MOE = 512
NF_MOE = D_FF // FC_MOE
NN_MOE = D_MODEL // FCB_MOE
TC_CMB = 128
VMEM_LIMIT = 56 * 1024 * 1024


def _cparams(sem):
    return pltpu.CompilerParams(dimension_semantics=sem, vmem_limit_bytes=VMEM_LIMIT)


def _rms_rows(x, g):
    ms = jnp.mean(x * x, axis=-1, keepdims=True)
    return x * lax.rsqrt(ms + EPS) * g


def _inproj_kernel(x_ref, g_ref, w_ref, o_ref, u_ref):
    @pl.when(pl.program_id(1) == 0)
    def _():
        def body(r, c):
            sl = pl.ds(pl.multiple_of(r * NORM_ROWS, NORM_ROWS), NORM_ROWS)
            u_ref[sl, :] = _rms_rows(x_ref[sl, :], g_ref[...]).astype(BF16)
            return c
        lax.fori_loop(0, TM_IN // NORM_ROWS, body, 0)

    o_ref[...] = jnp.dot(u_ref[...], w_ref[...], preferred_element_type=F32)


def _inproj(x2, g, w_pad):
    n_tok = x2.shape[0]
    return pl.pallas_call(
        _inproj_kernel,
        grid=(n_tok // TM_IN, IN_COLS_PAD // TN_IN),
        in_specs=[
            pl.BlockSpec((TM_IN, D_MODEL), lambda i, j: (i, 0)),
            pl.BlockSpec((1, D_MODEL), lambda i, j: (0, 0)),
            pl.BlockSpec((D_MODEL, TN_IN), lambda i, j: (0, j)),
        ],
        out_specs=pl.BlockSpec((TM_IN, TN_IN), lambda i, j: (i, j)),
        out_shape=jax.ShapeDtypeStruct((n_tok, IN_COLS_PAD), F32),
        scratch_shapes=[pltpu.VMEM((TM_IN, D_MODEL), BF16)],
        compiler_params=_cparams(("parallel", "arbitrary")),
        name="inproj",
    )(x2, g, w_pad)


def _hgrn_kernel(q_ref, f_ref, i_ref, g_ref, lb_ref, ng_ref, o_ref, st_ref, qd_ref, kd_ref, kr_ref, el_ref):
    C = HG_CHUNK
    nch = HG_TT // C
    ng = ng_ref[...]
    row = lax.broadcasted_iota(jnp.int32, (C, C), 0)
    col = lax.broadcasted_iota(jnp.int32, (C, C), 1)
    tril = row >= col
    tri_bf = jnp.where(tril, 1.0, 0.0).astype(BF16)
    lb = lb_ref[...]

    @pl.when(pl.program_id(1) == 0)
    def _():
        st_ref[...] = jnp.zeros_like(st_ref)

    for c in range(nch):
        sl = slice(c * C, (c + 1) * C)
        f = lb + (1.0 - lb) * jax.nn.sigmoid(f_ref[0, sl, :])
        logf = jnp.log(f)
        k = 1.0 - f
        hi = logf.astype(BF16)
        lo = (logf - hi.astype(F32)).astype(BF16)
        b = jnp.dot(tri_bf, hi, preferred_element_type=F32) + jnp.dot(tri_bf, lo, preferred_element_type=F32)
        e_last = jnp.exp(b[C - 1:C, :])
        k_dec = k * jnp.exp(-b)
        qd_ref[sl, :] = (jax.nn.silu(q_ref[0, sl, :]) * jnp.exp(b)).astype(BF16)
        kd_ref[sl, :] = k_dec.astype(BF16)
        kr_ref[sl, :] = (k_dec * e_last).astype(BF16)
        el_ref[c] = e_last

    for h in range(HG_HEADS):
        hs = slice(h * HG_DK, (h + 1) * HG_DK)
        st = st_ref[h]
        for c in range(nch):
            sl = slice(c * C, (c + 1) * C)
            q_dec = qd_ref[sl, hs]
            v_bf = i_ref[0, sl, hs].astype(BF16)
            a = lax.dot_general(q_dec, kd_ref[sl, hs], (((1,), (1,)), ((), ())), preferred_element_type=F32)
            a = jnp.where(tril, a, 0.0).astype(BF16)
            o = jnp.dot(a, v_bf, preferred_element_type=F32)
            o = o + lax.dot_general(q_dec, st.astype(BF16), (((1,), (1,)), ((), ())), preferred_element_type=F32)
            st = st * el_ref[c][:, hs] + lax.dot_general(v_bf, kr_ref[sl, hs], (((0,), (0,)), ((), ())),
                                                         preferred_element_type=F32)
            y = _rms_rows(o, ng) * jax.nn.silu(g_ref[0, sl, hs])
            o_ref[0, sl, hs] = y.astype(o_ref.dtype)
        st_ref[h] = st


def _hgrn(proj3, lb, ng):
    B, T, _ = proj3.shape

    def col(cb):
        return pl.BlockSpec((1, HG_TT, HG_WIDTH), lambda b, t, cb=cb: (b, t, cb // HG_HEADS))

    return pl.pallas_call(
        _hgrn_kernel,
        grid=(B, T // HG_TT),
        in_specs=[col(CB_HQ), col(CB_HF), col(CB_HI), col(CB_HG),
                  pl.BlockSpec((1, HG_WIDTH), lambda b, t: (0, 0)),
                  pl.BlockSpec((1, LANE), lambda b, t: (0, 0))],
        out_specs=pl.BlockSpec((1, HG_TT, HG_WIDTH), lambda b, t: (b, t, 0)),
        out_shape=jax.ShapeDtypeStruct((B, T, HG_WIDTH), BF16),
        scratch_shapes=[pltpu.VMEM((HG_HEADS, HG_DK, HG_DK), F32)] + [pltpu.VMEM((HG_TT, HG_WIDTH), BF16)] * 3
        + [pltpu.VMEM((HG_TT // HG_CHUNK, 1, HG_WIDTH), F32)],
        compiler_params=_cparams(("parallel", "arbitrary")),
        name="hgrn2",
    )(proj3, proj3, proj3, proj3, lb, ng)


def _gelu_tanh(x):
    return 0.5 * x * (1.0 + jnp.tanh(0.7978845608028654 * (x + 0.044715 * (x * x * x))))


def _compress_kernel(x_ref, pos_ref, w1_ref, w2_ref, o_ref):
    T = x_ref.shape[1]
    ng = T // CMP_STRIDE
    acc_a = jnp.zeros((ng, NSA_D), F32)
    acc_b = jnp.zeros((ng, NSA_D), F32)
    for r in range(CMP_STRIDE):
        xr = x_ref[0, pl.ds(r, ng, stride=CMP_STRIDE), :]
        xa = (xr + pos_ref[0, r:r + 1, :]).astype(BF16)
        xb = (xr + pos_ref[0, CMP_STRIDE + r:CMP_STRIDE + r + 1, :]).astype(BF16)
        acc_a = acc_a + jnp.dot(xa, w1_ref[0, r], preferred_element_type=F32)
        acc_b = acc_b + jnp.dot(xb, w1_ref[0, CMP_STRIDE + r], preferred_element_type=F32)
    hdn = _gelu_tanh(acc_a + pltpu.roll(acc_b, ng - 1, 0))
    o_ref[0, 0, 0] = jnp.dot(hdn.astype(BF16), w2_ref[0], preferred_element_type=F32)


def _compress(proj3, pos, w1r, w2b):
    B, T, _ = proj3.shape
    ng = T // CMP_STRIDE
    return pl.pallas_call(
        _compress_kernel,
        grid=(B, NSA_G, 2),
        in_specs=[
            pl.BlockSpec((1, T, LANE), lambda b, g, j: (b, 0, CB_KCM + 2 * j + g)),
            pl.BlockSpec((1, CMP_BLOCK, NSA_D), lambda b, g, j: (j, 0, 0)),
            pl.BlockSpec((1, CMP_BLOCK, NSA_D, NSA_D), lambda b, g, j: (j, 0, 0, 0)),
            pl.BlockSpec((1, NSA_D, NSA_D), lambda b, g, j: (j, 0, 0)),
        ],
        out_specs=pl.BlockSpec((1, 1, 1, ng, NSA_D), lambda b, g, j: (b, g, j, 0, 0)),
        out_shape=jax.ShapeDtypeStruct((B, NSA_G, 2, ng, NSA_D), F32),
        compiler_params=_cparams(("parallel", "parallel", "parallel")),
        name="nsa_compress",
    )(proj3, pos, w1r, w2b)


def _overlap_matrix(T):
    n_cmp = (T - CMP_BLOCK) // CMP_STRIDE + 1
    n_sel = T // SEL_BLOCK
    ratio_s = SEL_BLOCK // CMP_STRIDE
    ratio_c = CMP_BLOCK // CMP_STRIDE
    jj, mm, nn = np.meshgrid(np.arange(n_sel), np.arange(ratio_s), np.arange(ratio_c), indexing='ij')
    ii = ratio_s * jj + mm + nn - (ratio_c - 1)
    ok = (ii >= 0) & (ii < n_cmp)
    overlap = np.zeros((LANE, T // CMP_STRIDE), np.float32)
    np.add.at(overlap, (jj[ok], ii[ok]), 1.0)
    return overlap


MASK_OFF = -2.0 ** 100


def _expand_matrix(T):
    e = np.zeros((LANE, T), np.float32)
    e[np.arange(T) // SEL_BLOCK, np.arange(T)] = MASK_OFF
    return e


def _nsa_kernel(slopes_ref, q_ref, ks_ref, vs_ref, kw_ref, vw_ref, gt_ref, kvc_ref, ov_ref, e_ref, gq_ref, ng_ref,
                o_ref, ksb, vsb, kwb, vwb, bias_ref, s0_ref, s1_ref, p0_ref, p1_ref):
    T = ks_ref.shape[1]
    R, D = NSA_R, NSA_D
    N = R * TQ
    n_sel = T // SEL_BLOCK
    top_n = min(SEL_TOP, n_sel)
    g = pl.program_id(1)
    qi = pl.program_id(2)
    q0 = qi * TQ

    @pl.when(qi == 0)
    def _():
        cr = 256

        def body(r, c):
            r0 = pl.multiple_of(r * cr, cr)
            sl = pl.ds(r0, cr)
            kpos = r0 + lax.broadcasted_iota(jnp.int32, (cr, D), 0)
            lane_d = lax.broadcasted_iota(jnp.int32, (cr, D), 1)
            posf = jnp.where(lane_d == 0, kpos // SEL_BLOCK, jnp.where(lane_d == 1, kpos % SEL_BLOCK, 0))
            posf = posf.astype(F32).astype(BF16)
            ones = jnp.ones((cr, D), BF16)
            ksb[sl, 0:D] = ks_ref[0, sl, :].astype(BF16)
            ksb[sl, D:2 * D] = posf
            vsb[sl, 0:D] = vs_ref[0, sl, :].astype(BF16)
            vsb[sl, D:2 * D] = ones
            kwb[sl, 0:D] = kw_ref[0, sl, :].astype(BF16)
            kwb[sl, D:2 * D] = posf
            vwb[sl, 0:D] = vw_ref[0, sl, :].astype(BF16)
            vwb[sl, D:2 * D] = ones
            return c
        lax.fori_loop(0, T // cr, body, 0)
        bias_ref[T // KC] = jnp.full((TQ, KC), NEG, F32)

    def stack(x):
        return jnp.concatenate([x] * R, axis=0)

    q = q_ref[0] * (D ** -0.5)
    qs = jnp.concatenate([q[:, r * D:(r + 1) * D] for r in range(R)], axis=0).astype(BF16)
    slope = jnp.concatenate([jnp.full((TQ, 1), slopes_ref[g, r], F32) for r in range(R)], axis=0)
    lane_n = lax.broadcasted_iota(jnp.int32, (N, D), 1)
    slopef = jnp.where(lane_n == 0, slope * float(SEL_BLOCK), jnp.where(lane_n == 1, slope, 0.0))
    q_aug = jnp.concatenate([qs, slopef.astype(BF16)], axis=1)
    t1 = q0 + lax.broadcasted_iota(jnp.int32, (TQ, 1), 0)
    trow = stack(t1)

    kc = kvc_ref[0, 0, 0].astype(BF16)
    vc = kvc_ref[0, 0, 1].astype(BF16)
    ncmp = kc.shape[0]
    cend = lax.broadcasted_iota(jnp.int32, (1, ncmp), 1) * CMP_STRIDE + (CMP_BLOCK - 1)
    cend = jnp.where(cend < T, cend, 4 * T)
    valid_c = trow >= cend
    s = lax.dot_general(qs, kc, (((1,), (1,)), ((), ())), preferred_element_type=F32)
    s = s + slope * cend.astype(F32)
    s = jnp.where(valid_c, s, NEG)
    mx = jnp.max(s, axis=-1, keepdims=True)
    e = jnp.where(valid_c, jnp.exp(s - mx), 0.0)
    l = jnp.sum(e, axis=-1, keepdims=True)
    p_cmp = e / jnp.where(l > 0.0, l, 1.0)
    o_cmp = jnp.dot(p_cmp.astype(BF16), vc, preferred_element_type=F32)

    cend_t = lax.broadcasted_iota(jnp.int32, (ncmp, N), 0) * CMP_STRIDE + (CMP_BLOCK - 1)
    cend_t = jnp.where(cend_t < T, cend_t, 4 * T)
    t_t = q0 + (lax.broadcasted_iota(jnp.int32, (ncmp, N), 1) & (TQ - 1))
    valid_t = t_t >= cend_t
    slope_row = jnp.concatenate([jnp.full((1, TQ), slopes_ref[g, r], F32) for r in range(R)], axis=1)
    st = lax.dot_general(kc, qs, (((1,), (1,)), ((), ())), preferred_element_type=F32)
    st = jnp.where(valid_t, st + slope_row * cend_t.astype(F32), NEG)
    et = jnp.where(valid_t, jnp.exp(st - jnp.max(st, axis=0, keepdims=True)), 0.0)
    lt = jnp.sum(et, axis=0, keepdims=True)
    pt = et / jnp.where(lt > 0.0, lt, 1.0)
    psum = pt[:, 0:TQ]
    for r in range(1, R):
        psum = psum + pt[:, r * TQ:(r + 1) * TQ]
    ph = psum.astype(BF16)
    plo = (psum - ph.astype(F32)).astype(BF16)
    ovt = ov_ref[...]
    imp = (jnp.dot(ovt, ph, preferred_element_type=F32) + jnp.dot(ovt, plo, preferred_element_type=F32))[0:n_sel]

    blk = lax.broadcasted_iota(jnp.int32, (n_sel, TQ), 0)
    cur = (q0 + lax.broadcasted_iota(jnp.int32, (n_sel, TQ), 1)) // SEL_BLOCK
    forced = (blk == 0) | (blk == cur) | (blk == cur - 1)
    score = jnp.where(forced, FORCED_SCORE, jnp.where(blk <= cur, imp, -FORCED_SCORE))
    rank = jnp.zeros((n_sel, TQ), F32)
    for kk in range(n_sel):
        sk = score[kk:kk + 1, :]
        beats = jnp.where(sk > score, 1.0, jnp.where(sk == score, jnp.where(blk > kk, 1.0, 0.0), 0.0))
        rank = rank + beats
    notsel = jnp.where(rank < float(top_n), 0.0, 1.0).astype(BF16)
    nchunks = T // KC
    for c in range(nchunks):
        bias_ref[c] = lax.dot_general(notsel, e_ref[0:n_sel, c * KC:(c + 1) * KC], (((0,), (0,)), ((), ())),
                                      preferred_element_type=F32)
    cd = q0 // KC
    kpos = cd * KC + lax.broadcasted_iota(jnp.int32, (TQ, KC), 1)
    bias_ref[cd] = bias_ref[cd] + jnp.where(kpos <= t1, 0.0, NEG)

    start = pl.multiple_of(jnp.maximum(q0 - WINDOW, 0), TQ)
    dist = (t1 - start) - lax.broadcasted_iota(jnp.int32, (TQ, WSPAN), 1)
    bias_w = jnp.where(dist >= 0, jnp.where(dist < WINDOW, 0.0, NEG), NEG)
    kw = kwb[pl.ds(start, WSPAN), :]
    vw = vwb[pl.ds(start, WSPAN), :]
    o_win_parts = []
    for hf in range(NSA_SPLIT):
        rows = slice(hf * N // NSA_SPLIT, (hf + 1) * N // NSA_SPLIT)
        sw = lax.dot_general(q_aug[rows], kw, (((1,), (1,)), ((), ())), preferred_element_type=F32)
        sw = jnp.concatenate([sw[r * TQ:(r + 1) * TQ] + bias_w for r in range(R // NSA_SPLIT)], axis=0)
        pw = jnp.exp(sw - jnp.max(sw, axis=-1, keepdims=True))
        acc_w = jnp.dot(pw.astype(BF16), vw, preferred_element_type=F32)
        o_win_parts.append(acc_w[:, 0:D] / acc_w[:, D:D + 1])
    o_win = jnp.concatenate(o_win_parts, axis=0)

    n_ch = cd + 1

    def scores(c):
        k0 = pl.multiple_of(jnp.minimum(c, nchunks - 1) * KC, KC)
        return lax.dot_general(q_aug, ksb[pl.ds(k0, KC), :], (((1,), (1,)), ((), ())), preferred_element_type=F32)

    def soft(s_ref, c, m):
        bias = bias_ref[jnp.where(c < n_ch, c, nchunks)]
        sm = jnp.concatenate([s_ref[r * TQ:(r + 1) * TQ, :] + bias for r in range(R)], axis=0)
        m_new = jnp.maximum(m, jnp.max(sm, axis=-1, keepdims=True))
        return m_new, jnp.exp(m - m_new), jnp.exp(sm - m_new).astype(BF16)

    def pv(p_ref, c, alpha, acc):
        k0 = pl.multiple_of(jnp.minimum(c, nchunks - 1) * KC, KC)
        return alpha * acc + jnp.dot(p_ref[...], vsb[pl.ds(k0, KC), :], preferred_element_type=F32)

    s0_ref[...] = scores(0)
    m1, alpha1, p = soft(s0_ref, 0, jnp.full((N, 1), NEG, F32))
    p0_ref[...] = p
    s1_ref[...] = scores(1)

    def sel_body(jj, carry):
        m, alpha, acc = carry
        j = 2 * jj
        acc = pv(p0_ref, j, alpha, acc)
        m, alpha, p = soft(s1_ref, j + 1, m)
        p1_ref[...] = p
        s0_ref[...] = scores(j + 2)
        acc = pv(p1_ref, j + 1, alpha, acc)
        m, alpha, p = soft(s0_ref, j + 2, m)
        p0_ref[...] = p
        s1_ref[...] = scores(j + 3)
        return m, alpha, acc

    _, _, acc_sel = lax.fori_loop(0, (n_ch + 1) // 2, sel_body, (m1, alpha1, jnp.zeros((N, 2 * D), F32)))
    o_sel = acc_sel[:, 0:D] / acc_sel[:, D:D + 1]

    sig = jax.nn.sigmoid(gt_ref[0])
    sig_h = sig.astype(BF16)
    sig_l = (sig - sig_h.astype(F32)).astype(BF16)
    gq = gq_ref[0]
    gates = jnp.dot(sig_h, gq, preferred_element_type=F32) + jnp.dot(sig_l, gq, preferred_element_type=F32)
    parts = []
    for r in range(R):
        rows = slice(r * TQ, (r + 1) * TQ)
        g3 = [gates[:, (3 * r + br) * D:(3 * r + br + 1) * D] for br in range(3)]
        parts.append(g3[0] * o_cmp[rows] + g3[1] * o_sel[rows] + g3[2] * o_win[rows])
    o = jnp.concatenate(parts, axis=0)
    y = _rms_rows(o, ng_ref[...])
    for r in range(R):
        o_ref[0, :, r * D:(r + 1) * D] = y[r * TQ:(r + 1) * TQ].astype(o_ref.dtype)


def _gate_copy_matrix():
    m = np.zeros((NSA_G, LANE, 3 * NSA_R * NSA_D), np.float32)
    for g in range(NSA_G):
        for r in range(NSA_R):
            for br in range(3):
                m[g, (g * NSA_R + r) * 3 + br, (3 * r + br) * NSA_D:(3 * r + br + 1) * NSA_D] = 1.0
    return m


def _nsa(proj3, kvc, ng):
    B, T, _ = proj3.shape
    ncmp = T // CMP_STRIDE
    slopes = jnp.asarray(_alibi_slopes())
    ov = jnp.asarray(_overlap_matrix(T), BF16)
    ex = jnp.asarray(_expand_matrix(T), BF16)
    gq = jnp.asarray(_gate_copy_matrix(), BF16)

    def kv(cb):
        return pl.BlockSpec((1, T, LANE), lambda b, g, qi, s, cb=cb: (b, 0, cb + g))

    grid_spec = pltpu.PrefetchScalarGridSpec(
        num_scalar_prefetch=1,
        grid=(B, NSA_G, T // TQ),
        in_specs=[
            pl.BlockSpec((1, TQ, NSA_R * NSA_D), lambda b, g, qi, s: (b, qi, CB_NQ // NSA_R + g)),
            kv(CB_KSL), kv(CB_VSL), kv(CB_KWN), kv(CB_VWN),
            pl.BlockSpec((1, TQ, LANE), lambda b, g, qi, s: (b, qi, CB_GATE)),
            pl.BlockSpec((1, 1, 2, ncmp, NSA_D), lambda b, g, qi, s: (b, g, 0, 0, 0)),
            pl.BlockSpec((LANE, ncmp), lambda b, g, qi, s: (0, 0)),
            pl.BlockSpec((LANE, T), lambda b, g, qi, s: (0, 0)),
            pl.BlockSpec((1, LANE, 3 * NSA_R * NSA_D), lambda b, g, qi, s: (g, 0, 0)),
            pl.BlockSpec((1, NSA_D), lambda b, g, qi, s: (0, 0)),
        ],
        out_specs=pl.BlockSpec((1, TQ, NSA_R * NSA_D), lambda b, g, qi, s: (b, qi, g)),
        scratch_shapes=[pltpu.VMEM((T, 2 * NSA_D), BF16)] * 4 + [pltpu.VMEM((T // KC + 1, TQ, KC), F32)]
        + [pltpu.VMEM((NSA_R * TQ, KC), F32)] * 2 + [pltpu.VMEM((NSA_R * TQ, KC), BF16)] * 2,
    )
    return pl.pallas_call(
        _nsa_kernel,
        grid_spec=grid_spec,
        out_shape=jax.ShapeDtypeStruct((B, T, NSA_WIDTH), BF16),
        compiler_params=_cparams(("parallel", "parallel", "arbitrary")),
        name="nsa_attention",
    )(slopes, proj3, proj3, proj3, proj3, proj3, proj3, kvc, ov, ex, gq, ng)


def _outproj_kernel(hg_ref, ns_ref, x_ref, wo_ref, g2_ref, wr_ref, br_ref,
                    h1_ref, xs_ref, idx_ref, w_ref, xh_ref, xl_ref):
    h1_ref[...] = (x_ref[...]
                   + jnp.dot(hg_ref[...], wo_ref[0:HG_WIDTH, :], preferred_element_type=F32)
                   + jnp.dot(ns_ref[...], wo_ref[HG_WIDTH:, :], preferred_element_type=F32))

    def body(r, c):
        r0 = pl.multiple_of(r * NORM_ROWS, NORM_ROWS)
        xn = _rms_rows(h1_ref[pl.ds(r0, NORM_ROWS), :], g2_ref[...])
        xh = xn.astype(BF16)
        xh_ref[pl.ds(r0, NORM_ROWS), :] = xh
        xl_ref[pl.ds(r0, NORM_ROWS), :] = (xn - xh.astype(F32)).astype(BF16)
        for s in range(SEG):
            xs_ref[pl.ds(r0 * SEG + s, NORM_ROWS, stride=SEG), :] = xn[:, s * LANE:(s + 1) * LANE]
        return c
    lax.fori_loop(0, TM_OUT // NORM_ROWS, body, 0)

    hh = jnp.dot(xh_ref[...], wr_ref[...], preferred_element_type=F32)
    lh = jnp.dot(xl_ref[...], wr_ref[:, 0:LANE], preferred_element_type=F32)
    logits = hh[:, 0:LANE] + hh[:, LANE:2 * LANE] + lh + br_ref[...]
    lane = lax.broadcasted_iota(jnp.int32, logits.shape, 1)
    lg = jnp.where(lane < N_EXPERTS, logits, -jnp.inf)
    vals, idxs = [], []
    for _ in range(TOP_K):
        mx = jnp.max(lg, axis=-1, keepdims=True)
        ix = jnp.min(jnp.where(lg == mx, lane, LANE), axis=-1, keepdims=True)
        vals.append(mx)
        idxs.append(ix)
        lg = jnp.where(lane == ix, -jnp.inf, lg)
    es = [jnp.exp(v - vals[0]) for v in vals]
    den = es[0]
    for e in es[1:]:
        den = den + e
    idx_out = jnp.zeros(logits.shape, jnp.int32)
    w_out = jnp.zeros(logits.shape, F32)
    for k in range(TOP_K):
        idx_out = jnp.where(lane == k, idxs[k], idx_out)
        w_out = jnp.where(lane == k, es[k] / den, w_out)
    idx_ref[...] = idx_out
    w_ref[...] = w_out


def _outproj(y_hg, y_nsa, x2, wo, g2, wr_pad, br_pad):
    n_tok = x2.shape[0]
    row = lambda w: pl.BlockSpec((TM_OUT, w), lambda i: (i, 0))
    full = lambda a, b: pl.BlockSpec((a, b), lambda i: (0, 0))
    return pl.pallas_call(
        _outproj_kernel,
        grid=(n_tok // TM_OUT,),
        in_specs=[row(HG_WIDTH), row(NSA_WIDTH), row(D_MODEL), full(D_MODEL, D_MODEL), full(1, D_MODEL),
                  full(D_MODEL, 2 * LANE), full(1, LANE)],
        out_specs=[row(D_MODEL), pl.BlockSpec((TM_OUT * SEG, LANE), lambda i: (i, 0)), row(LANE), row(LANE)],
        out_shape=[jax.ShapeDtypeStruct((n_tok, D_MODEL), F32), jax.ShapeDtypeStruct((n_tok * SEG, LANE), F32),
                   jax.ShapeDtypeStruct((n_tok, LANE), jnp.int32), jax.ShapeDtypeStruct((n_tok, LANE), F32)],
        scratch_shapes=[pltpu.VMEM((TM_OUT, D_MODEL), BF16)] * 2,
        compiler_params=_cparams(("parallel",)),
        name="outproj_router",
    )(y_hg, y_nsa, x2, wo, g2, wr_pad, br_pad)


def _moe_kernel(e_ref, sbase_ref, rows_ref, nu_ref, stok_ref,
                xs_hbm, wg_ref, wu_ref, wd_ref, bg_ref, bu_ref, bd_ref,
                y_ref, xstage, xg, act, sem):
    i = pl.program_id(0)
    s = pl.program_id(1)
    nu = nu_ref[0]
    n_asg = stok_ref.shape[0]
    nq = (rows_ref[i] + RQ_MOE - 1) // RQ_MOE
    unroll = 8

    def row_copy(tok, r):
        return pltpu.make_async_copy(xs_hbm.at[pl.ds(tok * SEG, SEG)], xstage.at[pl.ds(r * SEG, SEG)], sem.at[0])

    def nrow_of(blk):
        return (rows_ref[blk] + RQ_MOE - 1) // RQ_MOE * RQ_MOE

    def start_row(base, r):
        row_copy(stok_ref[jnp.minimum(base + r, n_asg - 1)], r).start()

    def issue_range(blk, lo, hi):
        base = sbase_ref[blk]

        def body(r8, c):
            for u in range(unroll):
                start_row(base, r8 * unroll + u)
            return c
        lax.fori_loop(lo // unroll, hi // unroll, body, 0)

    def wait_rows(n):
        pltpu.make_async_copy(xs_hbm.at[pl.ds(0, n * SEG)], xstage.at[pl.ds(0, n * SEG)], sem.at[0]).wait()

    nrow_i = nq * RQ_MOE

    @pl.when(s == 0)
    def _():
        @pl.when(i == 0)
        def _():
            issue_range(0, 0, nrow_i)

        @pl.when(i < nu)
        def _():
            wait_rows(nrow_i)

    @pl.when((s == 1) & (i + 1 < nu))
    def _():
        issue_range(i + 1, 0, nrow_of(i + 1))

    def for_sub_blocks(fn):
        per = TS_MOE // RQ_MOE
        n_full = nq // per
        rem = nq % per

        def body(j, c):
            fn(pl.multiple_of(j * TS_MOE, TS_MOE), TS_MOE)
            return c
        lax.fori_loop(0, n_full, body, 0)
        r0 = n_full * TS_MOE
        size = TS_MOE // 2
        while size >= RQ_MOE:
            q = size // RQ_MOE

            @pl.when((rem & q) != 0)
            def _(r0=r0, size=size):
                fn(pl.multiple_of(r0, RQ_MOE), size)
            r0 = r0 + (rem & q) * RQ_MOE
            size //= 2

    @pl.when((s < NF_MOE) & (i < nu))
    def _():
        bg = bg_ref[0]
        bu = bu_ref[0]

        def phase_a(r0, size, from_stage):
            rs = pl.ds(r0, size)
            if from_stage:
                x = jnp.concatenate([xstage[pl.ds(r0 * SEG + sg, size, stride=SEG), :].astype(BF16)
                                     for sg in range(SEG)], axis=1)
                xg[rs, :] = x
            else:
                x = xg[rs, :]
            gate = jnp.dot(x, wg_ref[0].astype(BF16), preferred_element_type=F32) + bg
            up = jnp.dot(x, wu_ref[0].astype(BF16), preferred_element_type=F32) + bu
            gate = jnp.minimum(gate, SWIGLU_LIMIT)
            up = jnp.clip(up, -SWIGLU_LIMIT, SWIGLU_LIMIT)
            a = gate * jax.nn.sigmoid(SWIGLU_ALPHA * gate) * (up + 1.0)
            act[s, rs, :] = a.astype(BF16)

        @pl.when(s == 0)
        def _():
            for_sub_blocks(functools.partial(phase_a, from_stage=True))

        @pl.when(s > 0)
        def _():
            for_sub_blocks(functools.partial(phase_a, from_stage=False))

    @pl.when((s >= NF_MOE) & (i < nu))
    def _():
        bd = bd_ref[0]

        def phase_b(r0, size):
            rs = pl.ds(r0, size)
            a = jnp.concatenate([act[f, rs, :] for f in range(NF_MOE)], axis=1)
            y_ref[rs, :] = jnp.dot(a, wd_ref[0].astype(BF16), preferred_element_type=F32) + bd
        for_sub_blocks(phase_b)

        def zfill(j, c):
            y_ref[pl.ds(pl.multiple_of(j * RQ_MOE, RQ_MOE), RQ_MOE), :] = jnp.zeros((RQ_MOE, FCB_MOE), F32)
            return c
        lax.fori_loop(nq, SB_MOE // RQ_MOE, zfill, 0)

    @pl.when((s >= NF_MOE) & (i >= nu))
    def _():
        y_ref[...] = jnp.zeros_like(y_ref)


def _moe(sup_e, sup_base, sup_rows, n_used, stok, xs, wgu, wd, bgu, bd):
    ns = sup_e.shape[0]

    def e_of(i, e, nu):
        return e[jnp.where(i < nu[0], i, nu[0] - 1)]

    def f_of(i, s, nu):
        return jnp.where(i < nu[0], jnp.minimum(s, NF_MOE - 1), NF_MOE - 1)

    def n_of(i, s, nu):
        return jnp.where(i < nu[0], jnp.maximum(s - NF_MOE, 0), NN_MOE - 1)

    def wg_map(i, s, e, sb, rw, nu, tk):
        return (e_of(i, e, nu), 0, f_of(i, s, nu))

    def wu_map(i, s, e, sb, rw, nu, tk):
        return (e_of(i, e, nu), 0, NF_MOE + f_of(i, s, nu))

    def wd_map(i, s, e, sb, rw, nu, tk):
        return (e_of(i, e, nu), 0, n_of(i, s, nu))

    def y_map(i, s, e, sb, rw, nu, tk):
        return (i, jnp.maximum(s - NF_MOE, 0))

    grid_spec = pltpu.PrefetchScalarGridSpec(
        num_scalar_prefetch=5,
        grid=(ns, NF_MOE + NN_MOE),
        in_specs=[
            pl.BlockSpec(memory_space=pl.ANY),
            pl.BlockSpec((1, D_MODEL, FC_MOE), wg_map),
            pl.BlockSpec((1, D_MODEL, FC_MOE), wu_map),
            pl.BlockSpec((1, D_FF, FCB_MOE), wd_map),
            pl.BlockSpec((1, 1, FC_MOE), wg_map),
            pl.BlockSpec((1, 1, FC_MOE), wu_map),
            pl.BlockSpec((1, 1, FCB_MOE), wd_map),
        ],
        out_specs=pl.BlockSpec((SB_MOE, FCB_MOE), y_map),
        scratch_shapes=[pltpu.VMEM((SB_MOE * SEG, LANE), F32), pltpu.VMEM((SB_MOE, D_MODEL), BF16),
                        pltpu.VMEM((NF_MOE, SB_MOE, FC_MOE), BF16), pltpu.SemaphoreType.DMA((1,))],
    )
    return pl.pallas_call(
        _moe_kernel,
        grid_spec=grid_spec,
        out_shape=jax.ShapeDtypeStruct((ns * SB_MOE, D_MODEL), F32),
        compiler_params=_cparams(("arbitrary", "arbitrary")),
        name="moe_experts",
    )(sup_e, sup_base, sup_rows, n_used, stok, xs, wgu, wgu, wd, bgu, bgu, bd)


def _combine_kernel(pos_ref, y_hbm, h1_ref, w_ref, g_ref, o_ref, ybuf, sem):
    i = pl.program_id(0)
    n = pl.num_programs(0)
    slot = i % 2

    unroll = 4

    def row_copy(src, sl, k, r):
        return pltpu.make_async_copy(y_hbm.at[src], ybuf.at[sl, k * TC_CMB + r], sem.at[sl])

    def issue(blk, sl):
        def body(r4, c):
            for u in range(unroll):
                r = r4 * unroll + u
                for k in range(TOP_K):
                    row_copy(pos_ref[(blk * TC_CMB + r) * TOP_K + k], sl, k, r).start()
            return c
        lax.fori_loop(0, TC_CMB // unroll, body, 0)

    @pl.when(i == 0)
    def _():
        issue(0, 0)

    pltpu.make_async_copy(y_hbm.at[pl.ds(0, TOP_K * TC_CMB)], ybuf.at[slot], sem.at[slot]).wait()

    @pl.when(i + 1 < n)
    def _():
        issue(i + 1, 1 - slot)

    w = w_ref[...]
    h = h1_ref[...]
    for k in range(TOP_K):
        h = h + w[:, k:k + 1] * ybuf[slot, k * TC_CMB:(k + 1) * TC_CMB, :]
    o_ref[...] = _rms_rows(h, g_ref[...])


def _combine(pos, y_sorted, h1, top_w, g):
    n_tok = h1.shape[0]
    grid_spec = pltpu.PrefetchScalarGridSpec(
        num_scalar_prefetch=1,
        grid=(n_tok // TC_CMB,),
        in_specs=[
            pl.BlockSpec(memory_space=pl.ANY),
            pl.BlockSpec((TC_CMB, D_MODEL), lambda i, p: (i, 0)),
            pl.BlockSpec((TC_CMB, LANE), lambda i, p: (i, 0)),
            pl.BlockSpec((1, D_MODEL), lambda i, p: (0, 0)),
        ],
        out_specs=pl.BlockSpec((TC_CMB, D_MODEL), lambda i, p: (i, 0)),
        scratch_shapes=[pltpu.VMEM((2, TOP_K * TC_CMB, D_MODEL), F32), pltpu.SemaphoreType.DMA((2,))],
    )
    return pl.pallas_call(
        _combine_kernel,
        grid_spec=grid_spec,
        out_shape=jax.ShapeDtypeStruct((n_tok, D_MODEL), F32),
        compiler_params=_cparams(("arbitrary",)),
        name="moe_combine",
    )(pos, y_sorted, h1, top_w, g)


def _dispatch(top_idx):
    n_tok = top_idx.shape[0]
    n_asg = n_tok * TOP_K
    i32 = jnp.int32
    flat_e = top_idx.reshape(n_asg)
    order = jnp.argsort(flat_e).astype(i32)
    inv = jnp.argsort(order).astype(i32)
    stok = order // TOP_K
    eids = jnp.arange(N_EXPERTS, dtype=i32)
    counts = jnp.sum((flat_e[:, None] == eids[None, :]).astype(i32), axis=0)
    start = jnp.cumsum(counts) - counts
    nsup0 = (counts + SB_MOE - 1) // SB_MOE
    per = jnp.where(nsup0 > 0, (counts + nsup0 * PQ_MOE - 1) // jnp.maximum(nsup0 * PQ_MOE, 1) * PQ_MOE, 0)
    nsup = jnp.where(per > 0, (counts + per - 1) // jnp.maximum(per, 1), 0)
    sup_end = jnp.cumsum(nsup)
    sup_start = sup_end - nsup
    ns = -(-n_asg // SB_MOE) + N_EXPERTS
    ii = jnp.arange(ns, dtype=i32)
    sup_e = jnp.minimum(jnp.sum((sup_end[None, :] <= ii[:, None]).astype(i32), axis=1), N_EXPERTS - 1)
    k_i = ii - sup_start[sup_e]
    sup_base = (start[sup_e] + k_i * per[sup_e]).astype(i32)
    sup_rows = jnp.clip(counts[sup_e] - k_i * per[sup_e], 0, per[sup_e]).astype(i32)
    n_used = sup_end[-1:].astype(i32)
    rank = inv - start[flat_e]
    pe = jnp.maximum(per[flat_e], PQ_MOE)
    a = (rank // PQ_MOE).astype(F32)
    k = jnp.floor((a + 0.5) / (pe // PQ_MOE).astype(F32)).astype(i32)
    pos = ((sup_start[flat_e] + k) * SB_MOE + (rank - k * pe)).astype(i32)
    return sup_e.astype(i32), sup_base, sup_rows, n_used, stok, pos


def _alibi_slopes():
    h = np.arange(1, NSA_HEADS + 1, dtype=np.float32)
    return np.power(np.float32(2.0), -8.0 * h / NSA_HEADS).astype(np.float32).reshape(NSA_G, NSA_R)


def kernel(x, norm1_g, w_in, hg_lb_logits, hg_norm_g, cmp_pos, cmp_w1, cmp_w2, nsa_norm_g, w_out,
           norm2_g, w_router, b_router, w_gate_up, b_gate_up, w_down, b_down, final_norm_g):
    B, T, D = x.shape
    n_tok = B * T
    l = 0
    x2 = x.reshape(n_tok, D)

    w_in_pad = jnp.pad(w_in[l], ((0, 0), (0, IN_COLS_PAD - IN_COLS))).astype(BF16)
    lb_all = jnp.cumsum(jax.nn.softmax(hg_lb_logits.astype(F32), axis=0), axis=0)
    lb = lb_all[l].reshape(1, HG_WIDTH)
    w1r = cmp_w1[l].reshape(2, CMP_BLOCK, NSA_D, NSA_D).astype(BF16)
    w2b = cmp_w2[l].astype(BF16)
    wo = w_out[l].astype(BF16)
    wr_f = jnp.pad(w_router[l], ((0, 0), (0, LANE - N_EXPERTS)))
    wr_h = wr_f.astype(BF16)
    wr_pad = jnp.concatenate([wr_h, (wr_f - wr_h.astype(F32)).astype(BF16)], axis=1)
    br_pad = jnp.pad(b_router[l], (0, LANE - N_EXPERTS)).reshape(1, LANE)
    bgu = b_gate_up[l].reshape(N_EXPERTS, 1, 2 * D_FF)
    bd = b_down[l].reshape(N_EXPERTS, 1, D_MODEL)

    proj = _inproj(x2, norm1_g[l].reshape(1, D), w_in_pad)
    proj3 = proj.reshape(B, T, IN_COLS_PAD)
    y_hg = _hgrn(proj3, lb, hg_norm_g[l].reshape(1, HG_DK))
    kvc = _compress(proj3, cmp_pos[l], w1r, w2b)
    y_nsa = _nsa(proj3, kvc, nsa_norm_g[l].reshape(1, NSA_D))
    h1, xs, top_idx, top_w = _outproj(y_hg.reshape(n_tok, HG_WIDTH), y_nsa.reshape(n_tok, NSA_WIDTH), x2, wo,
                                      norm2_g[l].reshape(1, D), wr_pad, br_pad)
    sup_e, sup_base, sup_rows, n_used, stok, pos = _dispatch(top_idx[:, :TOP_K])
    y_sorted = _moe(sup_e, sup_base, sup_rows, n_used, stok, xs, w_gate_up[l], w_down[l], bgu, bd)
    out = _combine(pos, y_sorted, h1, top_w, final_norm_g.reshape(1, D))
    return out.reshape(B, T, D)
```

```python
import functools

import numpy as np
import jax
import jax.numpy as jnp
from jax import lax
from jax.experimental import pallas as pl
from jax.experimental.pallas import tpu as pltpu

F32 = jnp.float32
BF16 = jnp.bfloat16

D_MODEL = 2048
HG_WIDTH = 1024
HG_DK = 128
HG_HEADS = 8
HG_CHUNK = 64
NSA_WIDTH = 1024
NSA_D = 128
NSA_HEADS = 8
NSA_G = 2
NSA_R = 4
KV_WIDTH = NSA_G * NSA_D
CMP_BLOCK = 32
CMP_STRIDE = 16
SEL_BLOCK = 64
SEL_TOP = 16
WINDOW = 512
N_EXPERTS = 32
TOP_K = 4
D_FF = 2048
SWIGLU_LIMIT = 7.0
SWIGLU_ALPHA = 1.702
EPS = 1e-6
NEG = -1e30
FORCED_SCORE = 1e4
IN_COLS = 4 * HG_WIDTH + NSA_WIDTH + 6 * KV_WIDTH + 3 * NSA_HEADS

LANE = 128
CB_HQ, CB_HF, CB_HI, CB_HG = 0, 8, 16, 24
CB_NQ = 32
CB_KCM, CB_VCM, CB_KSL, CB_VSL, CB_KWN, CB_VWN = 40, 42, 44, 46, 48, 50
CB_GATE = 52

TM_IN = 1024
TN_IN = 768
IN_COLS_PAD = 9 * TN_IN
NORM_ROWS = 32
HG_TT = 256
TQ = 256
KC = 512
NSA_SPLIT = 2
WSPAN = WINDOW + TQ
TM_OUT = 512
SEG = D_MODEL // LANE
SB_MOE = 1536
PQ_MOE = 512
TS_MOE = 1024
RQ_MOE = 128
FC_MOE = 256
FCB_MOE = 512
NF_MOE = D_FF // FC_MOE
NN_MOE = D_MODEL // FCB_MOE
TC_CMB = 128
VMEM_LIMIT = 56 * 1024 * 1024


def _cparams(sem):
    return pltpu.CompilerParams(dimension_semantics=sem, vmem_limit_bytes=VMEM_LIMIT)


def _rms_rows(x, g):
    ms = jnp.mean(x * x, axis=-1, keepdims=True)
    return x * lax.rsqrt(ms + EPS) * g


def _inproj_kernel(x_ref, g_ref, w_ref, o_ref, u_ref):
    @pl.when(pl.program_id(1) == 0)
    def _():
        def body(r, c):
            sl = pl.ds(pl.multiple_of(r * NORM_ROWS, NORM_ROWS), NORM_ROWS)
            u_ref[sl, :] = _rms_rows(x_ref[sl, :], g_ref[...]).astype(BF16)
            return c
        lax.fori_loop(0, TM_IN // NORM_ROWS, body, 0)

    o_ref[...] = jnp.dot(u_ref[...], w_ref[...], preferred_element_type=F32)


def _inproj(x2, g, w_pad):
    n_tok = x2.shape[0]
    return pl.pallas_call(
        _inproj_kernel,
        grid=(n_tok // TM_IN, IN_COLS_PAD // TN_IN),
        in_specs=[
            pl.BlockSpec((TM_IN, D_MODEL), lambda i, j: (i, 0)),
            pl.BlockSpec((1, D_MODEL), lambda i, j: (0, 0)),
            pl.BlockSpec((D_MODEL, TN_IN), lambda i, j: (0, j)),
        ],
        out_specs=pl.BlockSpec((TM_IN, TN_IN), lambda i, j: (i, j)),
        out_shape=jax.ShapeDtypeStruct((n_tok, IN_COLS_PAD), F32),
        scratch_shapes=[pltpu.VMEM((TM_IN, D_MODEL), BF16)],
        compiler_params=_cparams(("parallel", "arbitrary")),
        name="inproj",
    )(x2, g, w_pad)


def _hgrn_kernel(q_ref, f_ref, i_ref, g_ref, lb_ref, ng_ref, o_ref, st_ref, qd_ref, kd_ref, kr_ref, el_ref):
    C = HG_CHUNK
    nch = HG_TT // C
    ng = ng_ref[...]
    row = lax.broadcasted_iota(jnp.int32, (C, C), 0)
    col = lax.broadcasted_iota(jnp.int32, (C, C), 1)
    tril = row >= col
    tri_bf = jnp.where(tril, 1.0, 0.0).astype(BF16)
    lb = lb_ref[...]

    @pl.when(pl.program_id(1) == 0)
    def _():
        st_ref[...] = jnp.zeros_like(st_ref)

    for c in range(nch):
        sl = slice(c * C, (c + 1) * C)
        f = lb + (1.0 - lb) * jax.nn.sigmoid(f_ref[0, sl, :])
        logf = jnp.log(f)
        k = 1.0 - f
        hi = logf.astype(BF16)
        lo = (logf - hi.astype(F32)).astype(BF16)
        b = jnp.dot(tri_bf, hi, preferred_element_type=F32) + jnp.dot(tri_bf, lo, preferred_element_type=F32)
        e_last = jnp.exp(b[C - 1:C, :])
        k_dec = k * jnp.exp(-b)
        qd_ref[sl, :] = (jax.nn.silu(q_ref[0, sl, :]) * jnp.exp(b)).astype(BF16)
        kd_ref[sl, :] = k_dec.astype(BF16)
        kr_ref[sl, :] = (k_dec * e_last).astype(BF16)
        el_ref[c] = e_last

    for h in range(HG_HEADS):
        hs = slice(h * HG_DK, (h + 1) * HG_DK)
        st = st_ref[h]
        for c in range(nch):
            sl = slice(c * C, (c + 1) * C)
            q_dec = qd_ref[sl, hs]
            v_bf = i_ref[0, sl, hs].astype(BF16)
            a = lax.dot_general(q_dec, kd_ref[sl, hs], (((1,), (1,)), ((), ())), preferred_element_type=F32)
            a = jnp.where(tril, a, 0.0).astype(BF16)
            o = jnp.dot(a, v_bf, preferred_element_type=F32)
            o = o + lax.dot_general(q_dec, st.astype(BF16), (((1,), (1,)), ((), ())), preferred_element_type=F32)
            st = st * el_ref[c][:, hs] + lax.dot_general(v_bf, kr_ref[sl, hs], (((0,), (0,)), ((), ())),
                                                         preferred_element_type=F32)
            y = _rms_rows(o, ng) * jax.nn.silu(g_ref[0, sl, hs])
            o_ref[0, sl, hs] = y.astype(o_ref.dtype)
        st_ref[h] = st


def _hgrn(proj3, lb, ng):
    B, T, _ = proj3.shape

    def col(cb):
        return pl.BlockSpec((1, HG_TT, HG_WIDTH), lambda b, t, cb=cb: (b, t, cb // HG_HEADS))

    return pl.pallas_call(
        _hgrn_kernel,
        grid=(B, T // HG_TT),
        in_specs=[col(CB_HQ), col(CB_HF), col(CB_HI), col(CB_HG),
                  pl.BlockSpec((1, HG_WIDTH), lambda b, t: (0, 0)),
                  pl.BlockSpec((1, LANE), lambda b, t: (0, 0))],
        out_specs=pl.BlockSpec((1, HG_TT, HG_WIDTH), lambda b, t: (b, t, 0)),
        out_shape=jax.ShapeDtypeStruct((B, T, HG_WIDTH), BF16),
        scratch_shapes=[pltpu.VMEM((HG_HEADS, HG_DK, HG_DK), F32)] + [pltpu.VMEM((HG_TT, HG_WIDTH), BF16)] * 3
        + [pltpu.VMEM((HG_TT // HG_CHUNK, 1, HG_WIDTH), F32)],
        compiler_params=_cparams(("parallel", "arbitrary")),
        name="hgrn2",
    )(proj3, proj3, proj3, proj3, lb, ng)


def _gelu_tanh(x):
    return 0.5 * x * (1.0 + jnp.tanh(0.7978845608028654 * (x + 0.044715 * (x * x * x))))


def _compress_kernel(x_ref, pos_ref, w1_ref, w2_ref, o_ref):
    T = x_ref.shape[1]
    ng = T // CMP_STRIDE
    acc_a = jnp.zeros((ng, NSA_D), F32)
    acc_b = jnp.zeros((ng, NSA_D), F32)
    for r in range(CMP_STRIDE):
        xr = x_ref[0, pl.ds(r, ng, stride=CMP_STRIDE), :]
        xa = (xr + pos_ref[0, r:r + 1, :]).astype(BF16)
        xb = (xr + pos_ref[0, CMP_STRIDE + r:CMP_STRIDE + r + 1, :]).astype(BF16)
        acc_a = acc_a + jnp.dot(xa, w1_ref[0, r], preferred_element_type=F32)
        acc_b = acc_b + jnp.dot(xb, w1_ref[0, CMP_STRIDE + r], preferred_element_type=F32)
    hdn = _gelu_tanh(acc_a + pltpu.roll(acc_b, ng - 1, 0))
    o_ref[0, 0, 0] = jnp.dot(hdn.astype(BF16), w2_ref[0], preferred_element_type=F32)


def _compress(proj3, pos, w1r, w2b):
    B, T, _ = proj3.shape
    ng = T // CMP_STRIDE
    return pl.pallas_call(
        _compress_kernel,
        grid=(B, NSA_G, 2),
        in_specs=[
            pl.BlockSpec((1, T, LANE), lambda b, g, j: (b, 0, CB_KCM + 2 * j + g)),
            pl.BlockSpec((1, CMP_BLOCK, NSA_D), lambda b, g, j: (j, 0, 0)),
            pl.BlockSpec((1, CMP_BLOCK, NSA_D, NSA_D), lambda b, g, j: (j, 0, 0, 0)),
            pl.BlockSpec((1, NSA_D, NSA_D), lambda b, g, j: (j, 0, 0)),
        ],
        out_specs=pl.BlockSpec((1, 1, 1, ng, NSA_D), lambda b, g, j: (b, g, j, 0, 0)),
        out_shape=jax.ShapeDtypeStruct((B, NSA_G, 2, ng, NSA_D), F32),
        compiler_params=_cparams(("parallel", "parallel", "parallel")),
        name="nsa_compress",
    )(proj3, pos, w1r, w2b)


def _overlap_matrix(T):
    n_cmp = (T - CMP_BLOCK) // CMP_STRIDE + 1
    n_sel = T // SEL_BLOCK
    ratio_s = SEL_BLOCK // CMP_STRIDE
    ratio_c = CMP_BLOCK // CMP_STRIDE
    jj, mm, nn = np.meshgrid(np.arange(n_sel), np.arange(ratio_s), np.arange(ratio_c), indexing='ij')
    ii = ratio_s * jj + mm + nn - (ratio_c - 1)
    ok = (ii >= 0) & (ii < n_cmp)
    overlap = np.zeros((LANE, T // CMP_STRIDE), np.float32)
    np.add.at(overlap, (jj[ok], ii[ok]), 1.0)
    return overlap


MASK_OFF = -2.0 ** 100


def _expand_matrix(T):
    e = np.zeros((LANE, T), np.float32)
    e[np.arange(T) // SEL_BLOCK, np.arange(T)] = MASK_OFF
    return e


def _nsa_kernel(slopes_ref, q_ref, ks_ref, vs_ref, kw_ref, vw_ref, gt_ref, kvc_ref, ov_ref, e_ref, gq_ref, ng_ref,
                o_ref, ksb, vsb, kwb, vwb, bias_ref, s0_ref, s1_ref, p0_ref, p1_ref):
    T = ks_ref.shape[1]
    R, D = NSA_R, NSA_D
    N = R * TQ
    n_sel = T // SEL_BLOCK
    top_n = min(SEL_TOP, n_sel)
    g = pl.program_id(1)
    qi = pl.program_id(2)
    q0 = qi * TQ

    @pl.when(qi == 0)
    def _():
        cr = 256

        def body(r, c):
            r0 = pl.multiple_of(r * cr, cr)
            sl = pl.ds(r0, cr)
            kpos = r0 + lax.broadcasted_iota(jnp.int32, (cr, D), 0)
            lane_d = lax.broadcasted_iota(jnp.int32, (cr, D), 1)
            posf = jnp.where(lane_d == 0, kpos // SEL_BLOCK, jnp.where(lane_d == 1, kpos % SEL_BLOCK, 0))
            posf = posf.astype(F32).astype(BF16)
            ones = jnp.ones((cr, D), BF16)
            ksb[sl, 0:D] = ks_ref[0, sl, :].astype(BF16)
            ksb[sl, D:2 * D] = posf
            vsb[sl, 0:D] = vs_ref[0, sl, :].astype(BF16)
            vsb[sl, D:2 * D] = ones
            kwb[sl, 0:D] = kw_ref[0, sl, :].astype(BF16)
            kwb[sl, D:2 * D] = posf
            vwb[sl, 0:D] = vw_ref[0, sl, :].astype(BF16)
            vwb[sl, D:2 * D] = ones
            return c
        lax.fori_loop(0, T // cr, body, 0)
        bias_ref[T // KC] = jnp.full((TQ, KC), NEG, F32)

    def stack(x):
        return jnp.concatenate([x] * R, axis=0)

    q = q_ref[0] * (D ** -0.5)
    qs = jnp.concatenate([q[:, r * D:(r + 1) * D] for r in range(R)], axis=0).astype(BF16)
    slope = jnp.concatenate([jnp.full((TQ, 1), slopes_ref[g, r], F32) for r in range(R)], axis=0)
    lane_n = lax.broadcasted_iota(jnp.int32, (N, D), 1)
    slopef = jnp.where(lane_n == 0, slope * float(SEL_BLOCK), jnp.where(lane_n == 1, slope, 0.0))
    q_aug = jnp.concatenate([qs, slopef.astype(BF16)], axis=1)
    t1 = q0 + lax.broadcasted_iota(jnp.int32, (TQ, 1), 0)
    trow = stack(t1)

    kc = kvc_ref[0, 0, 0].astype(BF16)
    vc = kvc_ref[0, 0, 1].astype(BF16)
    ncmp = kc.shape[0]
    cend = lax.broadcasted_iota(jnp.int32, (1, ncmp), 1) * CMP_STRIDE + (CMP_BLOCK - 1)
    cend = jnp.where(cend < T, cend, 4 * T)
    valid_c = trow >= cend
    s = lax.dot_general(qs, kc, (((1,), (1,)), ((), ())), preferred_element_type=F32)
    s = s + slope * cend.astype(F32)
    s = jnp.where(valid_c, s, NEG)
    mx = jnp.max(s, axis=-1, keepdims=True)
    e = jnp.where(valid_c, jnp.exp(s - mx), 0.0)
    l = jnp.sum(e, axis=-1, keepdims=True)
    p_cmp = e / jnp.where(l > 0.0, l, 1.0)
    o_cmp = jnp.dot(p_cmp.astype(BF16), vc, preferred_element_type=F32)

    cend_t = lax.broadcasted_iota(jnp.int32, (ncmp, N), 0) * CMP_STRIDE + (CMP_BLOCK - 1)
    cend_t = jnp.where(cend_t < T, cend_t, 4 * T)
    t_t = q0 + (lax.broadcasted_iota(jnp.int32, (ncmp, N), 1) & (TQ - 1))
    valid_t = t_t >= cend_t
    slope_row = jnp.concatenate([jnp.full((1, TQ), slopes_ref[g, r], F32) for r in range(R)], axis=1)
    st = lax.dot_general(kc, qs, (((1,), (1,)), ((), ())), preferred_element_type=F32)
    st = jnp.where(valid_t, st + slope_row * cend_t.astype(F32), NEG)
    et = jnp.where(valid_t, jnp.exp(st - jnp.max(st, axis=0, keepdims=True)), 0.0)
    lt = jnp.sum(et, axis=0, keepdims=True)
    pt = et / jnp.where(lt > 0.0, lt, 1.0)
    psum = pt[:, 0:TQ]
    for r in range(1, R):
        psum = psum + pt[:, r * TQ:(r + 1) * TQ]
    ph = psum.astype(BF16)
    plo = (psum - ph.astype(F32)).astype(BF16)
    ovt = ov_ref[...]
    imp = (jnp.dot(ovt, ph, preferred_element_type=F32) + jnp.dot(ovt, plo, preferred_element_type=F32))[0:n_sel]

    blk = lax.broadcasted_iota(jnp.int32, (n_sel, TQ), 0)
    cur = (q0 + lax.broadcasted_iota(jnp.int32, (n_sel, TQ), 1)) // SEL_BLOCK
    forced = (blk == 0) | (blk == cur) | (blk == cur - 1)
    score = jnp.where(forced, FORCED_SCORE, jnp.where(blk <= cur, imp, -FORCED_SCORE))
    rank = jnp.zeros((n_sel, TQ), F32)
    for kk in range(n_sel):
        sk = score[kk:kk + 1, :]
        beats = jnp.where(sk > score, 1.0, jnp.where(sk == score, jnp.where(blk > kk, 1.0, 0.0), 0.0))
        rank = rank + beats
    notsel = jnp.where(rank < float(top_n), 0.0, 1.0).astype(BF16)
    nchunks = T // KC
    for c in range(nchunks):
        bias_ref[c] = lax.dot_general(notsel, e_ref[0:n_sel, c * KC:(c + 1) * KC], (((0,), (0,)), ((), ())),
                                      preferred_element_type=F32)
    cd = q0 // KC
    kpos = cd * KC + lax.broadcasted_iota(jnp.int32, (TQ, KC), 1)
    bias_ref[cd] = bias_ref[cd] + jnp.where(kpos <= t1, 0.0, NEG)

    start = pl.multiple_of(jnp.maximum(q0 - WINDOW, 0), TQ)
    dist = (t1 - start) - lax.broadcasted_iota(jnp.int32, (TQ, WSPAN), 1)
    bias_w = jnp.where(dist >= 0, jnp.where(dist < WINDOW, 0.0, NEG), NEG)
    kw = kwb[pl.ds(start, WSPAN), :]
    vw = vwb[pl.ds(start, WSPAN), :]
    o_win_parts = []
    for hf in range(NSA_SPLIT):
        rows = slice(hf * N // NSA_SPLIT, (hf + 1) * N // NSA_SPLIT)
        sw = lax.dot_general(q_aug[rows], kw, (((1,), (1,)), ((), ())), preferred_element_type=F32)
        sw = jnp.concatenate([sw[r * TQ:(r + 1) * TQ] + bias_w for r in range(R // NSA_SPLIT)], axis=0)
        pw = jnp.exp((sw - jnp.max(sw, axis=-1, keepdims=True)).astype(BF16))
        acc_w = jnp.dot(pw, vw, preferred_element_type=F32)
        o_win_parts.append(acc_w[:, 0:D] / acc_w[:, D:D + 1])
    o_win = jnp.concatenate(o_win_parts, axis=0)

    n_ch = cd + 1

    def scores(c):
        k0 = pl.multiple_of(jnp.minimum(c, nchunks - 1) * KC, KC)
        return lax.dot_general(q_aug, ksb[pl.ds(k0, KC), :], (((1,), (1,)), ((), ())), preferred_element_type=F32)

    def soft(s_ref, c, m):
        bias = bias_ref[jnp.where(c < n_ch, c, nchunks)]
        sm = jnp.concatenate([s_ref[r * TQ:(r + 1) * TQ, :] + bias for r in range(R)], axis=0)
        m_new = jnp.maximum(m, jnp.max(sm, axis=-1, keepdims=True))
        return m_new, jnp.exp(m - m_new), jnp.exp((sm - m_new).astype(BF16))

    def pv(p_ref, c, alpha, acc):
        k0 = pl.multiple_of(jnp.minimum(c, nchunks - 1) * KC, KC)
        return alpha * acc + jnp.dot(p_ref[...], vsb[pl.ds(k0, KC), :], preferred_element_type=F32)

    s0_ref[...] = scores(0)
    m1, alpha1, p = soft(s0_ref, 0, jnp.full((N, 1), NEG, F32))
    p0_ref[...] = p
    s1_ref[...] = scores(1)

    def sel_body(jj, carry):
        m, alpha, acc = carry
        j = 2 * jj
        acc = pv(p0_ref, j, alpha, acc)
        m, alpha, p = soft(s1_ref, j + 1, m)
        p1_ref[...] = p
        s0_ref[...] = scores(j + 2)
        acc = pv(p1_ref, j + 1, alpha, acc)
        m, alpha, p = soft(s0_ref, j + 2, m)
        p0_ref[...] = p
        s1_ref[...] = scores(j + 3)
        return m, alpha, acc

    _, _, acc_sel = lax.fori_loop(0, (n_ch + 1) // 2, sel_body, (m1, alpha1, jnp.zeros((N, 2 * D), F32)))
    o_sel = acc_sel[:, 0:D] / acc_sel[:, D:D + 1]

    sig = jax.nn.sigmoid(gt_ref[0])
    sig_h = sig.astype(BF16)
    sig_l = (sig - sig_h.astype(F32)).astype(BF16)
    gq = gq_ref[0]
    gates = jnp.dot(sig_h, gq, preferred_element_type=F32) + jnp.dot(sig_l, gq, preferred_element_type=F32)
    parts = []
    for r in range(R):
        rows = slice(r * TQ, (r + 1) * TQ)
        g3 = [gates[:, (3 * r + br) * D:(3 * r + br + 1) * D] for br in range(3)]
        parts.append(g3[0] * o_cmp[rows] + g3[1] * o_sel[rows] + g3[2] * o_win[rows])
    o = jnp.concatenate(parts, axis=0)
    y = _rms_rows(o, ng_ref[...])
    for r in range(R):
        o_ref[0, :, r * D:(r + 1) * D] = y[r * TQ:(r + 1) * TQ].astype(o_ref.dtype)


def _gate_copy_matrix():
    m = np.zeros((NSA_G, LANE, 3 * NSA_R * NSA_D), np.float32)
    for g in range(NSA_G):
        for r in range(NSA_R):
            for br in range(3):
                m[g, (g * NSA_R + r) * 3 + br, (3 * r + br) * NSA_D:(3 * r + br + 1) * NSA_D] = 1.0
    return m


def _nsa(proj3, kvc, ng):
    B, T, _ = proj3.shape
    ncmp = T // CMP_STRIDE
    slopes = jnp.asarray(_alibi_slopes())
    ov = jnp.asarray(_overlap_matrix(T), BF16)
    ex = jnp.asarray(_expand_matrix(T), BF16)
    gq = jnp.asarray(_gate_copy_matrix(), BF16)

    def kv(cb):
        return pl.BlockSpec((1, T, LANE), lambda b, g, qi, s, cb=cb: (b, 0, cb + g))

    grid_spec = pltpu.PrefetchScalarGridSpec(
        num_scalar_prefetch=1,
        grid=(B, NSA_G, T // TQ),
        in_specs=[
            pl.BlockSpec((1, TQ, NSA_R * NSA_D), lambda b, g, qi, s: (b, qi, CB_NQ // NSA_R + g)),
            kv(CB_KSL), kv(CB_VSL), kv(CB_KWN), kv(CB_VWN),
            pl.BlockSpec((1, TQ, LANE), lambda b, g, qi, s: (b, qi, CB_GATE)),
            pl.BlockSpec((1, 1, 2, ncmp, NSA_D), lambda b, g, qi, s: (b, g, 0, 0, 0)),
            pl.BlockSpec((LANE, ncmp), lambda b, g, qi, s: (0, 0)),
            pl.BlockSpec((LANE, T), lambda b, g, qi, s: (0, 0)),
            pl.BlockSpec((1, LANE, 3 * NSA_R * NSA_D), lambda b, g, qi, s: (g, 0, 0)),
            pl.BlockSpec((1, NSA_D), lambda b, g, qi, s: (0, 0)),
        ],
        out_specs=pl.BlockSpec((1, TQ, NSA_R * NSA_D), lambda b, g, qi, s: (b, qi, g)),
        scratch_shapes=[pltpu.VMEM((T, 2 * NSA_D), BF16)] * 4 + [pltpu.VMEM((T // KC + 1, TQ, KC), F32)]
        + [pltpu.VMEM((NSA_R * TQ, KC), F32)] * 2 + [pltpu.VMEM((NSA_R * TQ, KC), BF16)] * 2,
    )
    return pl.pallas_call(
        _nsa_kernel,
        grid_spec=grid_spec,
        out_shape=jax.ShapeDtypeStruct((B, T, NSA_WIDTH), BF16),
        compiler_params=_cparams(("parallel", "parallel", "arbitrary")),
        name="nsa_attention",
    )(slopes, proj3, proj3, proj3, proj3, proj3, proj3, kvc, ov, ex, gq, ng)


def _outproj_kernel(hg_ref, ns_ref, x_ref, wo_ref, g2_ref, wr_ref, br_ref,
                    h1_ref, xs_ref, idx_ref, w_ref, xh_ref, xl_ref):
    h1_ref[...] = (x_ref[...]
                   + jnp.dot(hg_ref[...], wo_ref[0:HG_WIDTH, :], preferred_element_type=F32)
                   + jnp.dot(ns_ref[...], wo_ref[HG_WIDTH:, :], preferred_element_type=F32))

    def body(r, c):
        r0 = pl.multiple_of(r * NORM_ROWS, NORM_ROWS)
        xn = _rms_rows(h1_ref[pl.ds(r0, NORM_ROWS), :], g2_ref[...])
        xh = xn.astype(BF16)
        xh_ref[pl.ds(r0, NORM_ROWS), :] = xh
        xl_ref[pl.ds(r0, NORM_ROWS), :] = (xn - xh.astype(F32)).astype(BF16)
        for s in range(SEG):
            xs_ref[pl.ds(r0 * SEG + s, NORM_ROWS, stride=SEG), :] = xn[:, s * LANE:(s + 1) * LANE]
        return c
    lax.fori_loop(0, TM_OUT // NORM_ROWS, body, 0)

    hh = jnp.dot(xh_ref[...], wr_ref[...], preferred_element_type=F32)
    lh = jnp.dot(xl_ref[...], wr_ref[:, 0:LANE], preferred_element_type=F32)
    logits = hh[:, 0:LANE] + hh[:, LANE:2 * LANE] + lh + br_ref[...]
    lane = lax.broadcasted_iota(jnp.int32, logits.shape, 1)
    lg = jnp.where(lane < N_EXPERTS, logits, -jnp.inf)
    vals, idxs = [], []
    for _ in range(TOP_K):
        mx = jnp.max(lg, axis=-1, keepdims=True)
        ix = jnp.min(jnp.where(lg == mx, lane, LANE), axis=-1, keepdims=True)
        vals.append(mx)
        idxs.append(ix)
        lg = jnp.where(lane == ix, -jnp.inf, lg)
    es = [jnp.exp(v - vals[0]) for v in vals]
    den = es[0]
    for e in es[1:]:
        den = den + e
    idx_out = jnp.zeros(logits.shape, jnp.int32)
    w_out = jnp.zeros(logits.shape, F32)
    for k in range(TOP_K):
        idx_out = jnp.where(lane == k, idxs[k], idx_out)
        w_out = jnp.where(lane == k, es[k] / den, w_out)
    idx_ref[...] = idx_out
    w_ref[...] = w_out


def _outproj(y_hg, y_nsa, x2, wo, g2, wr_pad, br_pad):
    n_tok = x2.shape[0]
    row = lambda w: pl.BlockSpec((TM_OUT, w), lambda i: (i, 0))
    full = lambda a, b: pl.BlockSpec((a, b), lambda i: (0, 0))
    return pl.pallas_call(
        _outproj_kernel,
        grid=(n_tok // TM_OUT,),
        in_specs=[row(HG_WIDTH), row(NSA_WIDTH), row(D_MODEL), full(D_MODEL, D_MODEL), full(1, D_MODEL),
                  full(D_MODEL, 2 * LANE), full(1, LANE)],
        out_specs=[row(D_MODEL), pl.BlockSpec((TM_OUT * SEG, LANE), lambda i: (i, 0)), row(LANE), row(LANE)],
        out_shape=[jax.ShapeDtypeStruct((n_tok, D_MODEL), F32), jax.ShapeDtypeStruct((n_tok * SEG, LANE), F32),
                   jax.ShapeDtypeStruct((n_tok, LANE), jnp.int32), jax.ShapeDtypeStruct((n_tok, LANE), F32)],
        scratch_shapes=[pltpu.VMEM((TM_OUT, D_MODEL), BF16)] * 2,
        compiler_params=_cparams(("parallel",)),
        name="outproj_router",
    )(y_hg, y_nsa, x2, wo, g2, wr_pad, br_pad)


def _moe_kernel(e_ref, sbase_ref, rows_ref, nu_ref, stok_ref,
                xs_hbm, wg_ref, wu_ref, wd_ref, bg_ref, bu_ref, bd_ref,
                y_ref, xstage, xg, act, sem):
    i = pl.program_id(0)
    s = pl.program_id(1)
    nu = nu_ref[0]
    n_asg = stok_ref.shape[0]
    nq = (rows_ref[i] + RQ_MOE - 1) // RQ_MOE
    unroll = 8

    def row_copy(tok, r):
        return pltpu.make_async_copy(xs_hbm.at[pl.ds(tok * SEG, SEG)], xstage.at[pl.ds(r * SEG, SEG)], sem.at[0])

    def nrow_of(blk):
        return (rows_ref[blk] + RQ_MOE - 1) // RQ_MOE * RQ_MOE

    def start_row(base, r):
        row_copy(stok_ref[jnp.minimum(base + r, n_asg - 1)], r).start()

    def issue_range(blk, lo, hi):
        base = sbase_ref[blk]

        def body(r8, c):
            for u in range(unroll):
                start_row(base, r8 * unroll + u)
            return c
        lax.fori_loop(lo // unroll, hi // unroll, body, 0)

    def wait_rows(n):
        pltpu.make_async_copy(xs_hbm.at[pl.ds(0, n * SEG)], xstage.at[pl.ds(0, n * SEG)], sem.at[0]).wait()

    nrow_i = nq * RQ_MOE

    @pl.when(s == 0)
    def _():
        @pl.when(i == 0)
        def _():
            issue_range(0, 0, nrow_i)

        @pl.when(i < nu)
        def _():
            wait_rows(nrow_i)

    @pl.when((s == 1) & (i + 1 < nu))
    def _():
        issue_range(i + 1, 0, nrow_of(i + 1))

    def for_sub_blocks(fn):
        per = TS_MOE // RQ_MOE
        n_full = nq // per
        rem = nq % per

        def body(j, c):
            fn(pl.multiple_of(j * TS_MOE, TS_MOE), TS_MOE)
            return c
        lax.fori_loop(0, n_full, body, 0)
        r0 = n_full * TS_MOE
        size = TS_MOE // 2
        while size >= RQ_MOE:
            q = size // RQ_MOE

            @pl.when((rem & q) != 0)
            def _(r0=r0, size=size):
                fn(pl.multiple_of(r0, RQ_MOE), size)
            r0 = r0 + (rem & q) * RQ_MOE
            size //= 2

    @pl.when((s < NF_MOE) & (i < nu))
    def _():
        bg = bg_ref[0]
        bu = bu_ref[0]

        def phase_a(r0, size, from_stage):
            rs = pl.ds(r0, size)
            if from_stage:
                x = jnp.concatenate([xstage[pl.ds(r0 * SEG + sg, size, stride=SEG), :].astype(BF16)
                                     for sg in range(SEG)], axis=1)
                xg[rs, :] = x
            else:
                x = xg[rs, :]
            gate = jnp.dot(x, wg_ref[0].astype(BF16), preferred_element_type=F32) + bg
            up = jnp.dot(x, wu_ref[0].astype(BF16), preferred_element_type=F32) + bu
            gate = jnp.minimum(gate, SWIGLU_LIMIT)
            up = jnp.clip(up, -SWIGLU_LIMIT, SWIGLU_LIMIT)
            a = gate * jax.nn.sigmoid(SWIGLU_ALPHA * gate) * (up + 1.0)
            act[s, rs, :] = a.astype(BF16)

        @pl.when(s == 0)
        def _():
            for_sub_blocks(functools.partial(phase_a, from_stage=True))

        @pl.when(s > 0)
        def _():
            for_sub_blocks(functools.partial(phase_a, from_stage=False))

    @pl.when((s >= NF_MOE) & (i < nu))
    def _():
        bd = bd_ref[0]

        def phase_b(r0, size):
            rs = pl.ds(r0, size)
            a = jnp.concatenate([act[f, rs, :] for f in range(NF_MOE)], axis=1)
            y_ref[rs, :] = jnp.dot(a, wd_ref[0].astype(BF16), preferred_element_type=F32) + bd
        for_sub_blocks(phase_b)

        def zfill(j, c):
            y_ref[pl.ds(pl.multiple_of(j * RQ_MOE, RQ_MOE), RQ_MOE), :] = jnp.zeros((RQ_MOE, FCB_MOE), F32)
            return c
        lax.fori_loop(nq, SB_MOE // RQ_MOE, zfill, 0)

    @pl.when((s >= NF_MOE) & (i >= nu))
    def _():
        y_ref[...] = jnp.zeros_like(y_ref)


def _moe(sup_e, sup_base, sup_rows, n_used, stok, xs, wgu, wd, bgu, bd):
    ns = sup_e.shape[0]

    def e_of(i, e, nu):
        return e[jnp.where(i < nu[0], i, nu[0] - 1)]

    def f_of(i, s, nu):
        return jnp.where(i < nu[0], jnp.minimum(s, NF_MOE - 1), NF_MOE - 1)

    def n_of(i, s, nu):
        return jnp.where(i < nu[0], jnp.maximum(s - NF_MOE, 0), NN_MOE - 1)

    def wg_map(i, s, e, sb, rw, nu, tk):
        return (e_of(i, e, nu), 0, f_of(i, s, nu))

    def wu_map(i, s, e, sb, rw, nu, tk):
        return (e_of(i, e, nu), 0, NF_MOE + f_of(i, s, nu))

    def wd_map(i, s, e, sb, rw, nu, tk):
        return (e_of(i, e, nu), 0, n_of(i, s, nu))

    def y_map(i, s, e, sb, rw, nu, tk):
        return (i, jnp.maximum(s - NF_MOE, 0))

    grid_spec = pltpu.PrefetchScalarGridSpec(
        num_scalar_prefetch=5,
        grid=(ns, NF_MOE + NN_MOE),
        in_specs=[
            pl.BlockSpec(memory_space=pl.ANY),
            pl.BlockSpec((1, D_MODEL, FC_MOE), wg_map),
            pl.BlockSpec((1, D_MODEL, FC_MOE), wu_map),
            pl.BlockSpec((1, D_FF, FCB_MOE), wd_map),
            pl.BlockSpec((1, 1, FC_MOE), wg_map),
            pl.BlockSpec((1, 1, FC_MOE), wu_map),
            pl.BlockSpec((1, 1, FCB_MOE), wd_map),
        ],
        out_specs=pl.BlockSpec((SB_MOE, FCB_MOE), y_map),
        scratch_shapes=[pltpu.VMEM((SB_MOE * SEG, LANE), F32), pltpu.VMEM((SB_MOE, D_MODEL), BF16),
                        pltpu.VMEM((NF_MOE, SB_MOE, FC_MOE), BF16), pltpu.SemaphoreType.DMA((1,))],
    )
    return pl.pallas_call(
        _moe_kernel,
        grid_spec=grid_spec,
        out_shape=jax.ShapeDtypeStruct((ns * SB_MOE, D_MODEL), F32),
        compiler_params=_cparams(("arbitrary", "arbitrary")),
        name="moe_experts",
    )(sup_e, sup_base, sup_rows, n_used, stok, xs, wgu, wgu, wd, bgu, bgu, bd)


def _combine_kernel(pos_ref, y_hbm, h1_ref, w_ref, g_ref, o_ref, ybuf, sem):
    i = pl.program_id(0)
    n = pl.num_programs(0)
    slot = i % 2

    unroll = 4

    def row_copy(src, sl, k, r):
        return pltpu.make_async_copy(y_hbm.at[src], ybuf.at[sl, k * TC_CMB + r], sem.at[sl])

    def issue(blk, sl):
        def body(r4, c):
            for u in range(unroll):
                r = r4 * unroll + u
                for k in range(TOP_K):
                    row_copy(pos_ref[(blk * TC_CMB + r) * TOP_K + k], sl, k, r).start()
            return c
        lax.fori_loop(0, TC_CMB // unroll, body, 0)

    @pl.when(i == 0)
    def _():
        issue(0, 0)

    pltpu.make_async_copy(y_hbm.at[pl.ds(0, TOP_K * TC_CMB)], ybuf.at[slot], sem.at[slot]).wait()

    @pl.when(i + 1 < n)
    def _():
        issue(i + 1, 1 - slot)

    w = w_ref[...]
    h = h1_ref[...]
    for k in range(TOP_K):
        h = h + w[:, k:k + 1] * ybuf[slot, k * TC_CMB:(k + 1) * TC_CMB, :]
    o_ref[...] = _rms_rows(h, g_ref[...])


def _combine(pos, y_sorted, h1, top_w, g):
    n_tok = h1.shape[0]
    grid_spec = pltpu.PrefetchScalarGridSpec(
        num_scalar_prefetch=1,
        grid=(n_tok // TC_CMB,),
        in_specs=[
            pl.BlockSpec(memory_space=pl.ANY),
            pl.BlockSpec((TC_CMB, D_MODEL), lambda i, p: (i, 0)),
            pl.BlockSpec((TC_CMB, LANE), lambda i, p: (i, 0)),
            pl.BlockSpec((1, D_MODEL), lambda i, p: (0, 0)),
        ],
        out_specs=pl.BlockSpec((TC_CMB, D_MODEL), lambda i, p: (i, 0)),
        scratch_shapes=[pltpu.VMEM((2, TOP_K * TC_CMB, D_MODEL), F32), pltpu.SemaphoreType.DMA((2,))],
    )
    return pl.pallas_call(
        _combine_kernel,
        grid_spec=grid_spec,
        out_shape=jax.ShapeDtypeStruct((n_tok, D_MODEL), F32),
        compiler_params=_cparams(("arbitrary",)),
        name="moe_combine",
    )(pos, y_sorted, h1, top_w, g)


def _dispatch(top_idx):
    n_tok = top_idx.shape[0]
    n_asg = n_tok * TOP_K
    i32 = jnp.int32
    flat_e = top_idx.reshape(n_asg)
    order = jnp.argsort(flat_e).astype(i32)
    inv = jnp.argsort(order).astype(i32)
    stok = order // TOP_K
    eids = jnp.arange(N_EXPERTS, dtype=i32)
    counts = jnp.sum((flat_e[:, None] == eids[None, :]).astype(i32), axis=0)
    start = jnp.cumsum(counts) - counts
    nsup0 = (counts + SB_MOE - 1) // SB_MOE
    per = jnp.where(nsup0 > 0, (counts + nsup0 * PQ_MOE - 1) // jnp.maximum(nsup0 * PQ_MOE, 1) * PQ_MOE, 0)
    nsup = jnp.where(per > 0, (counts + per - 1) // jnp.maximum(per, 1), 0)
    sup_end = jnp.cumsum(nsup)
    sup_start = sup_end - nsup
    ns = -(-n_asg // SB_MOE) + N_EXPERTS
    ii = jnp.arange(ns, dtype=i32)
    sup_e = jnp.minimum(jnp.sum((sup_end[None, :] <= ii[:, None]).astype(i32), axis=1), N_EXPERTS - 1)
    k_i = ii - sup_start[sup_e]
    sup_base = (start[sup_e] + k_i * per[sup_e]).astype(i32)
    sup_rows = jnp.clip(counts[sup_e] - k_i * per[sup_e], 0, per[sup_e]).astype(i32)
    n_used = sup_end[-1:].astype(i32)
    rank = inv - start[flat_e]
    pe = jnp.maximum(per[flat_e], PQ_MOE)
    a = (rank // PQ_MOE).astype(F32)
    k = jnp.floor((a + 0.5) / (pe // PQ_MOE).astype(F32)).astype(i32)
    pos = ((sup_start[flat_e] + k) * SB_MOE + (rank - k * pe)).astype(i32)
    return sup_e.astype(i32), sup_base, sup_rows, n_used, stok, pos


def _alibi_slopes():
    h = np.arange(1, NSA_HEADS + 1, dtype=np.float32)
    return np.power(np.float32(2.0), -8.0 * h / NSA_HEADS).astype(np.float32).reshape(NSA_G, NSA_R)


def kernel(x, norm1_g, w_in, hg_lb_logits, hg_norm_g, cmp_pos, cmp_w1, cmp_w2, nsa_norm_g, w_out,
           norm2_g, w_router, b_router, w_gate_up, b_gate_up, w_down, b_down, final_norm_g):
    B, T, D = x.shape
    n_tok = B * T
    l = 0
    x2 = x.reshape(n_tok, D)

    w_in_pad = jnp.pad(w_in[l], ((0, 0), (0, IN_COLS_PAD - IN_COLS))).astype(BF16)
    lb_all = jnp.cumsum(jax.nn.softmax(hg_lb_logits.astype(F32), axis=0), axis=0)
    lb = lb_all[l].reshape(1, HG_WIDTH)
    w1r = cmp_w1[l].reshape(2, CMP_BLOCK, NSA_D, NSA_D).astype(BF16)
    w2b = cmp_w2[l].astype(BF16)
    wo = w_out[l].astype(BF16)
    wr_f = jnp.pad(w_router[l], ((0, 0), (0, LANE - N_EXPERTS)))
    wr_h = wr_f.astype(BF16)
    wr_pad = jnp.concatenate([wr_h, (wr_f - wr_h.astype(F32)).astype(BF16)], axis=1)
    br_pad = jnp.pad(b_router[l], (0, LANE - N_EXPERTS)).reshape(1, LANE)
    bgu = b_gate_up[l].reshape(N_EXPERTS, 1, 2 * D_FF)
    bd = b_down[l].reshape(N_EXPERTS, 1, D_MODEL)

    proj = _inproj(x2, norm1_g[l].reshape(1, D), w_in_pad)
    proj3 = proj.reshape(B, T, IN_COLS_PAD)
    y_hg = _hgrn(proj3, lb, hg_norm_g[l].reshape(1, HG_DK))
    kvc = _compress(proj3, cmp_pos[l], w1r, w2b)
    y_nsa = _nsa(proj3, kvc, nsa_norm_g[l].reshape(1, NSA_D))
    h1, xs, top_idx, top_w = _outproj(y_hg.reshape(n_tok, HG_WIDTH), y_nsa.reshape(n_tok, NSA_WIDTH), x2, wo,
                                      norm2_g[l].reshape(1, D), wr_pad, br_pad)
    sup_e, sup_base, sup_rows, n_used, stok, pos = _dispatch(top_idx[:, :TOP_K])
    y_sorted = _moe(sup_e, sup_base, sup_rows, n_used, stok, xs, w_gate_up[l], w_down[l], bgu, bd)
    out = _combine(pos, y_sorted, h1, top_w, final_norm_g.reshape(1, D))
    return out.reshape(B, T, D)
```
